```python
import math
import jax, jax.numpy as jnp
from jax import lax

D_MODEL = 2048
BATCH = 2
SEQ = 4096
DEPTH = 1
DEC_BATCH = 32
DEC_SEQ = 1
PAST_LEN = 8192
PAGE_SIZE = 128

H_SB = 8
HD_SB = 128
SB_W = H_SB * HD_SB
SB_QBLOCK = 128
SB_BIAS_HI = -2.0
SB_BIAS_LO = -12.0
H_GDN = 8
DK_GDN = 128
DV_GDN = 128
GDN_QK = H_GDN * DK_GDN
GDN_V = H_GDN * DV_GDN
CONV_W = 4
CONV_DIM = 2 * GDN_QK + GDN_V
GDN_CHUNK = 64
N_EXPERTS = 32
TOP_K = 4
D_FF = D_MODEL
SWIGLU_ALPHA = 1.702
SWIGLU_LIMIT = 7.0
PLE_DIM = 256
EPS = 1e-6
IN_SPLITS = (SB_W, SB_W, SB_W, CONV_DIM, GDN_V, H_GDN, H_GDN, D_MODEL, D_MODEL)
IN_W = 3 * SB_W + CONV_DIM + GDN_V + 2 * H_GDN + 2 * D_MODEL

kernel_name = 'stickbreak_gdn_moe_ple_decode_step'


def rmsnorm(x, g):
    xf = x.astype(jnp.float32)
    y = xf * lax.rsqrt(jnp.mean(xf * xf, axis=-1, keepdims=True) + EPS)
    return (y * g.astype(jnp.float32)).astype(x.dtype)


def l2norm(x):
    return x * lax.rsqrt(jnp.sum(x * x, axis=-1, keepdims=True) + EPS)


def split_cols(t, sizes):
    out, start = [], 0
    for s in sizes:
        out.append(t[..., start:start + s])
        start += s
    return out


def stick_breaking(q, k, v, q_start, bias):
    B, Lq, H, D = q.shape
    Lk = k.shape[1]
    C = SB_QBLOCK if Lq % SB_QBLOCK == 0 else Lq
    nb = Lq // C
    k_pos = jnp.arange(Lk)
    scale = D ** -0.5
    hb = bias.astype(jnp.float32)[None, :, None, None]

    def block(args):
        qb, qpos = args
        z = jnp.einsum('bqhd,bkhd->bhqk', qb, k).astype(jnp.float32) * scale + hb
        visible = k_pos[None, :] < qpos[:, None]
        log_keep = jnp.where(visible, jax.nn.log_sigmoid(-z), 0.0)
        between = lax.cumsum(log_keep, axis=3, reverse=True) - log_keep
        w = jnp.where(visible, jnp.exp(jax.nn.log_sigmoid(z) + between), 0.0)
        return jnp.einsum('bhqk,bkhd->bqhd', w.astype(v.dtype), v)

    qb = jnp.moveaxis(q.reshape(B, nb, C, H, D), 1, 0)
    qpos = (q_start + jnp.arange(Lq)).reshape(nb, C)
    o = lax.map(block, (qb, qpos))
    return jnp.moveaxis(o, 0, 1).reshape(B, Lq, H, D)


def causal_conv(x, hist, w):
    L = x.shape[1]
    xp = jnp.concatenate([hist.astype(x.dtype), x], axis=1)
    out = xp[:, 0:L] * w[0]
    for j in range(1, CONV_W):
        out = out + xp[:, j:j + L] * w[j]
    return out, xp[:, L:]


def gated_delta_rule(q, k, v, g, beta, s0):
    B, L, H, DK = q.shape
    DV = v.shape[-1]
    C = GDN_CHUNK if L % GDN_CHUNK == 0 else L
    n = L // C

    def chunks(t):
        t = t.reshape((B, n, C, H) + t.shape[3:])
        return jnp.moveaxis(t, (1, 3), (0, 2))

    qc = chunks(q * (DK ** -0.5))
    kc, vc, bc = chunks(k), chunks(v), chunks(beta)
    gc = jnp.cumsum(chunks(g), axis=-1)
    idx = jnp.arange(C)
    lower_incl = idx[:, None] >= idx[None, :]
    strict = idx[:, None] > idx[None, :]
    decay = jnp.exp(jnp.where(lower_incl, gc[..., :, None] - gc[..., None, :], -jnp.inf))
    kb = kc * bc[..., None]
    kk = jnp.einsum('nbhid,nbhjd->nbhij', kb, kc) * decay
    a_mat = jnp.eye(C, dtype=jnp.float32) + jnp.where(strict, kk, 0.0)
    rhs = jnp.concatenate([vc * bc[..., None], kb * jnp.exp(gc)[..., None]], axis=-1)
    sol = lax.linalg.triangular_solve(a_mat, rhs, left_side=True, lower=True, unit_diagonal=True)
    u, w = sol[..., :DV], sol[..., DV:]
    qk = jnp.einsum('nbhid,nbhjd->nbhij', qc, kc) * decay
    q_dec = qc * jnp.exp(gc)[..., None]
    k_to_end = kc * jnp.exp(gc[..., -1:] - gc)[..., None]

    def step(S, xs):
        u_c, w_c, qk_c, qd_c, ke_c, g_last = xs
        v_new = u_c - jnp.einsum('bhcd,bhde->bhce', w_c, S)
        o = jnp.einsum('bhcd,bhde->bhce', qd_c, S) + jnp.einsum('bhij,bhje->bhie', qk_c, v_new)
        S = S * jnp.exp(g_last)[..., None, None] + jnp.einsum('bhcd,bhce->bhde', ke_c, v_new)
        return S, o

    S, o = lax.scan(step, s0, (u, w, qk, q_dec, k_to_end, gc[..., -1]))
    o = jnp.moveaxis(o, (0, 2), (1, 3)).reshape(B, L, H, DV)
    return o, S


def token_mix(h, k_past, v_past, conv_hist, ssm0, g_mix, w_in, b_sb, w_conv, a_log, dt_bias,
              g_onorm, w_o_sb, w_o_gdn, w_out):
    B, L, _ = h.shape
    u = rmsnorm(h, g_mix)
    q_sb, k_sb, v_sb, conv_in, z, b_lg, a_lg, gt_sb, gt_gdn = split_cols(u @ w_in, IN_SPLITS)
    heads = lambda t, d: t.reshape(B, L, -1, d)
    k_sb, v_sb = heads(k_sb, HD_SB), heads(v_sb, HD_SB)
    if k_past is None:
        k_all, v_all = k_sb, v_sb
    else:
        k_all = jnp.concatenate([k_past.astype(k_sb.dtype), k_sb], axis=1)
        v_all = jnp.concatenate([v_past.astype(v_sb.dtype), v_sb], axis=1)
    o_sb = stick_breaking(heads(q_sb, HD_SB), k_all, v_all, k_all.shape[1] - L, b_sb)

    conv_out, conv_state = causal_conv(conv_in, conv_hist, w_conv)
    conv_out = jax.nn.silu(conv_out.astype(jnp.float32))
    q_g, k_g, v_g = split_cols(conv_out, (GDN_QK, GDN_QK, GDN_V))
    q_g, k_g, v_g = l2norm(heads(q_g, DK_GDN)), l2norm(heads(k_g, DK_GDN)), heads(v_g, DV_GDN)
    beta = jax.nn.sigmoid(b_lg.astype(jnp.float32))
    g = -jnp.exp(a_log.astype(jnp.float32)) * jax.nn.softplus(a_lg.astype(jnp.float32) + dt_bias.astype(jnp.float32))
    o_g, ssm_state = gated_delta_rule(q_g, k_g, v_g, g, beta, ssm0.astype(jnp.float32))
    o_g = rmsnorm(o_g, g_onorm) * jax.nn.silu(heads(z, DV_GDN).astype(jnp.float32))
    o_g = o_g.astype(h.dtype).reshape(B, L, GDN_V)

    merged = (jax.nn.sigmoid(gt_sb) * (o_sb.reshape(B, L, SB_W) @ w_o_sb)
              + jax.nn.sigmoid(gt_gdn) * (o_g @ w_o_gdn))
    return h + merged @ w_out, k_sb, v_sb, conv_state, ssm_state.astype(h.dtype)


def moe(u, w_router, b_router, w_up, b_up, w_down, b_down):
    logits = (u @ w_router + b_router).astype(jnp.float32)
    top_val, top_idx = lax.top_k(logits, TOP_K)
    top_w = jax.nn.softmax(top_val, axis=-1)
    combine = jnp.einsum('tk,tke->te', top_w, jax.nn.one_hot(top_idx, N_EXPERTS, dtype=jnp.float32))
    out = jnp.zeros(u.shape, jnp.float32)
    for e in range(N_EXPERTS):
        hid = u @ w_up[e] + b_up[e]
        glu = jnp.minimum(hid[:, :D_FF], SWIGLU_LIMIT)
        lin = jnp.clip(hid[:, D_FF:], -SWIGLU_LIMIT, SWIGLU_LIMIT)
        act = glu * jax.nn.sigmoid(SWIGLU_ALPHA * glu) * (lin + 1.0)
        out = out + combine[:, e:e + 1] * (act @ w_down[e] + b_down[e])
    return out.astype(u.dtype)


def setup_inputs(seed: int = 0) -> dict:
    key = jax.random.key(seed)
    ks = iter(jax.random.split(key, 40))
    nrm = lambda shape, s: jax.random.normal(next(ks), shape, jnp.float32) * s
    n_pages = PAST_LEN // PAGE_SIZE
    n_used = DEC_BATCH * n_pages
    n_pool = n_used + n_used // 4
    page_table = jax.random.permutation(next(ks), n_pool)[:n_used].reshape(DEC_BATCH, n_pages).astype(jnp.int32)
    dt = jnp.exp(jax.random.uniform(next(ks), (DEPTH, H_GDN), jnp.float32, math.log(1e-3), math.log(1e-1)))
    dt_bias = dt + jnp.log(-jnp.expm1(-dt))
    a_log = jnp.log(jax.random.uniform(next(ks), (DEPTH, H_GDN), jnp.float32, 1.0, 16.0))
    b_sb = jnp.broadcast_to(jnp.linspace(SB_BIAS_HI, SB_BIAS_LO, H_SB, dtype=jnp.float32), (DEPTH, H_SB)) + nrm((DEPTH, H_SB), 0.1)
    gain = lambda shape: 1.0 + nrm(shape, 0.01)
    return {
        'x_prompt': nrm((BATCH, SEQ, D_MODEL), 1.0),
        'x_sample': nrm((DEC_BATCH, DEC_SEQ, D_MODEL), 1.0),
        'cache_k': nrm((DEPTH, n_pool, PAGE_SIZE, H_SB, HD_SB), 1.0),
        'cache_v': nrm((DEPTH, n_pool, PAGE_SIZE, H_SB, HD_SB), 1.0),
        'page_table': page_table,
        'state_conv': nrm((DEPTH, DEC_BATCH, CONV_W - 1, CONV_DIM), 1.0),
        'state_ssm': nrm((DEPTH, DEC_BATCH, H_GDN, DK_GDN, DV_GDN), 0.1),
        'p_prompt': nrm((DEPTH, BATCH, SEQ, PLE_DIM), 1.0),
        'p_sample': nrm((DEPTH, DEC_BATCH, DEC_SEQ, PLE_DIM), 1.0),
        'g_mix': gain((DEPTH, D_MODEL)),
        'w_in': nrm((DEPTH, D_MODEL, IN_W), D_MODEL ** -0.5),
        'b_sb': b_sb,
        'w_conv': nrm((DEPTH, CONV_W, CONV_DIM), CONV_W ** -0.5),
        'a_log': a_log,
        'dt_bias': dt_bias,
        'g_onorm': gain((DEPTH, DV_GDN)),
        'w_o_sb': nrm((DEPTH, SB_W, D_MODEL), SB_W ** -0.5),
        'w_o_gdn': nrm((DEPTH, GDN_V, D_MODEL), GDN_V ** -0.5),
        'w_out': nrm((DEPTH, D_MODEL, D_MODEL), D_MODEL ** -0.5),
        'g_ffn': gain((DEPTH, D_MODEL)),
        'w_router': nrm((DEPTH, D_MODEL, N_EXPERTS), D_MODEL ** -0.5),
        'b_router': nrm((DEPTH, N_EXPERTS), 0.01),
        'w_up': nrm((DEPTH, N_EXPERTS, D_MODEL, 2 * D_FF), D_MODEL ** -0.5),
        'b_up': nrm((DEPTH, N_EXPERTS, 2 * D_FF), 0.01),
        'w_down': nrm((DEPTH, N_EXPERTS, D_FF, D_MODEL), D_FF ** -0.5),
        'b_down': nrm((DEPTH, N_EXPERTS, D_MODEL), 0.01),
        'g_ple': gain((DEPTH, D_MODEL)),
        'w_ple_gate': nrm((DEPTH, D_MODEL, D_MODEL), D_MODEL ** -0.5),
        'w_ple_proj': nrm((DEPTH, PLE_DIM, D_MODEL), PLE_DIM ** -0.5),
        'g_final': gain((D_MODEL,)),
    }


def reference(x_prompt, x_sample, cache_k, cache_v, page_table, state_conv, state_ssm, p_prompt, p_sample,
              g_mix, w_in, b_sb, w_conv, a_log, dt_bias, g_onorm, w_o_sb, w_o_gdn, w_out, g_ffn, w_router, b_router,
              w_up, b_up, w_down, b_down, g_ple, w_ple_gate, w_ple_proj, g_final):
    Bp, Sp, D = x_prompt.shape
    Bs, Ss, _ = x_sample.shape
    n_pages = page_table.shape[1]
    hp, hs = x_prompt, x_sample
    kp_l, vp_l, ks_l, vs_l, cp_l, sp_l, cs_l, ss_l = [], [], [], [], [], [], [], []
    for i in range(DEPTH):
        lw = (g_mix[i], w_in[i], b_sb[i], w_conv[i], a_log[i], dt_bias[i], g_onorm[i], w_o_sb[i], w_o_gdn[i], w_out[i])
        conv0 = jnp.zeros((Bp, CONV_W - 1, CONV_DIM), x_prompt.dtype)
        ssm0 = jnp.zeros((Bp, H_GDN, DK_GDN, DV_GDN), jnp.float32)
        hp, kp, vp, cp, sp = token_mix(hp, None, None, conv0, ssm0, *lw)
        k_past = cache_k[i][page_table].reshape(Bs, n_pages * PAGE_SIZE, H_SB, HD_SB)
        v_past = cache_v[i][page_table].reshape(Bs, n_pages * PAGE_SIZE, H_SB, HD_SB)
        hs, kn, vn, cn, sn = token_mix(hs, k_past, v_past, state_conv[i], state_ssm[i], *lw)
        h = jnp.concatenate([hp.reshape(-1, D), hs.reshape(-1, D)], axis=0)
        p = jnp.concatenate([p_prompt[i].reshape(-1, PLE_DIM), p_sample[i].reshape(-1, PLE_DIM)], axis=0)
        h = h + moe(rmsnorm(h, g_ffn[i]), w_router[i], b_router[i], w_up[i], b_up[i], w_down[i], b_down[i])
        gate = jax.nn.sigmoid(rmsnorm(h, g_ple[i]) @ w_ple_gate[i])
        h = h + gate * (p.astype(h.dtype) @ w_ple_proj[i])
        hp = h[:Bp * Sp].reshape(Bp, Sp, D)
        hs = h[Bp * Sp:].reshape(Bs, Ss, D)
        kp_l.append(kp); vp_l.append(vp); ks_l.append(kn); vs_l.append(vn)
        cp_l.append(cp); sp_l.append(sp); cs_l.append(cn); ss_l.append(sn)
    y_prompt = rmsnorm(hp, g_final)
    y_sample = rmsnorm(hs, g_final)
    return (y_prompt, y_sample, jnp.stack(kp_l), jnp.stack(vp_l), jnp.stack(ks_l), jnp.stack(vs_l),
            jnp.stack(cp_l), jnp.stack(sp_l), jnp.stack(cs_l), jnp.stack(ss_l))
```

```python
import functools

import jax
import jax.numpy as jnp
from jax import lax
from jax.experimental import pallas as pl
from jax.experimental.pallas import tpu as pltpu

F32 = jnp.float32
BF16 = jnp.bfloat16
I32 = jnp.int32

EPS = 1e-6
LANES = 128
HEAD_DIM = 128
N_HEADS = 8
CONV_TAPS = 4
GDN_CHUNK = 64
TOP_K = 4
SWIGLU_ALPHA = 1.702
SWIGLU_LIMIT = 7.0
VMEM_LIMIT = 56 * 1024 * 1024

ROW_TILE = 256
MM_ROW_TILE = 512
MM_COL_TILE = 1024
ATT_TILE = 256
MOE_TILE_ROWS = 1024
MOE_SUB_ROWS = 256
MOE_FF_TILE = 256


def _cparams(sem):
    return pltpu.CompilerParams(dimension_semantics=sem, vmem_limit_bytes=VMEM_LIMIT)


def _sigmoid(x):
    return 1.0 / (1.0 + jnp.exp(-x))


def _softplus_neg_abs(x):
    return jnp.log(1.0 + jnp.exp(-jnp.abs(x)))


def _split2(x):
    hi = x.astype(BF16)
    lo = (x - hi.astype(F32)).astype(BF16)
    return hi, lo


def _split3(x):
    hi = x.astype(BF16)
    r = x - hi.astype(F32)
    mid = r.astype(BF16)
    lo = (r - mid.astype(F32)).astype(BF16)
    return hi, mid, lo


def _dot(a, b):
    return jnp.dot(a, b, preferred_element_type=F32)


def _dot_nt(a, b):
    return lax.dot_general(a, b, (((1,), (1,)), ((), ())), preferred_element_type=F32)


def _dot_tn(a, b):
    return lax.dot_general(a, b, (((0,), (0,)), ((), ())), preferred_element_type=F32)


def _dot_wide(a, b):
    ah, al = _split2(a)
    bh, bl = _split2(b)
    return _dot(ah, bh) + (_dot(ah, bl) + _dot(al, bh))


def _dot_exact_lhs(a_bf16, x):
    hi, mid, lo = _split3(x)
    return _dot(a_bf16, hi) + (_dot(a_bf16, mid) + _dot(a_bf16, lo))


def _rms(x, g):
    return x * lax.rsqrt(jnp.mean(x * x, axis=-1, keepdims=True) + EPS) * g


def _rms_kernel(x_ref, g_ref, o_ref):
    o_ref[...] = _rms(x_ref[...], g_ref[...]).astype(o_ref.dtype)


def rmsnorm_cast(x, g, tm):
    m, d = x.shape
    return pl.pallas_call(
        _rms_kernel,
        grid=(m // tm,),
        in_specs=[pl.BlockSpec((tm, d), lambda i: (i, 0)), pl.BlockSpec((1, d), lambda i: (0, 0))],
        out_specs=pl.BlockSpec((tm, d), lambda i: (i, 0)),
        out_shape=jax.ShapeDtypeStruct((m, d), BF16),
        compiler_params=_cparams(("arbitrary",)),
        name="rmsnorm_cast",
    )(x, g.reshape(1, d))


def _mm_kernel(a_ref, w_ref, o_ref):
    o_ref[...] = _dot(a_ref[...], w_ref[...]).astype(o_ref.dtype)


def matmul(a, w, tm, tn, name):
    m, k = a.shape
    n = w.shape[1]
    return pl.pallas_call(
        _mm_kernel,
        grid=(n // tn, m // tm),
        in_specs=[pl.BlockSpec((tm, k), lambda j, i: (i, 0)), pl.BlockSpec((k, tn), lambda j, i: (0, j))],
        out_specs=pl.BlockSpec((tm, tn), lambda j, i: (i, j)),
        out_shape=jax.ShapeDtypeStruct((m, n), F32),
        compiler_params=_cparams(("arbitrary", "arbitrary")),
        name=name,
    )(a, w)


def _suffix_matrix():
    j = lax.broadcasted_iota(I32, (LANES, 2 * LANES), 0)
    s = lax.broadcasted_iota(I32, (LANES, 2 * LANES), 1)
    return jnp.where((s >= LANES) | (j > s), 1.0, 0.0).astype(BF16)


def _sb_prompt_kernel(bias_ref, q_ref, k_ref, v_ref, cum_ref, o_ref, kb_scr, vb_scr, *, tile, scale):
    h = pl.program_id(1)
    i = pl.program_id(2)

    @pl.when(i == 0)
    def _():
        kb_scr[...] = k_ref[...].astype(BF16)
        vb_scr[...] = v_ref[...].astype(BF16)

    q = q_ref[...].astype(BF16)
    bias = bias_ref[h]
    cum = cum_ref[...]
    rows = lax.broadcasted_iota(I32, (tile, LANES), 0)
    cols = lax.broadcasted_iota(I32, (tile, LANES), 1)
    nsub = tile // LANES

    def body(jj, carry):
        run, acc = carry
        j = i - jj
        k0 = pl.multiple_of(j * tile, tile)
        s = _dot_nt(q, kb_scr[pl.ds(k0, tile), :]) * scale + bias
        off = (j - i) * tile
        for sb in reversed(range(nsub)):
            ssub = s[:, sb * LANES:(sb + 1) * LANES]
            vis = (cols + (off + sb * LANES)) < rows
            t = _softplus_neg_abs(ssub)
            log_beta = jnp.minimum(ssub, 0.0) - t
            log_keep = jnp.where(vis, -jnp.maximum(ssub, 0.0) - t, 0.0)
            hi, lo = _split2(log_keep)
            c = _dot(hi, cum) + _dot(lo, cum)
            w = jnp.where(vis, jnp.exp(log_beta + (c[:, :LANES] + run)), 0.0)
            run = run + c[:, LANES:]
            ks = pl.multiple_of(k0 + sb * LANES, LANES)
            acc = acc + _dot(w.astype(BF16), vb_scr[pl.ds(ks, LANES), :])
        return run, acc

    zero = jnp.zeros((tile, LANES), F32)
    _, acc = lax.fori_loop(0, i + 1, body, (zero, zero))
    o_ref[...] = acc.astype(o_ref.dtype)


def sb_attention_prompt(proj, bias, batch, seq, q_col, k_col, v_col):
    tile = ATT_TILE
    nq = seq // tile
    kern = functools.partial(_sb_prompt_kernel, tile=tile, scale=HEAD_DIM ** -0.5)
    return pl.pallas_call(
        kern,
        grid_spec=pltpu.PrefetchScalarGridSpec(
            num_scalar_prefetch=0,
            grid=(batch, N_HEADS, nq),
            in_specs=[
                pl.BlockSpec(memory_space=pltpu.SMEM),
                pl.BlockSpec((tile, HEAD_DIM), lambda b, h, i: (b * nq + i, q_col + h)),
                pl.BlockSpec((seq, HEAD_DIM), lambda b, h, i: (b, k_col + h)),
                pl.BlockSpec((seq, HEAD_DIM), lambda b, h, i: (b, v_col + h)),
                pl.BlockSpec((LANES, 2 * LANES), lambda b, h, i: (0, 0)),
            ],
            out_specs=pl.BlockSpec((tile, HEAD_DIM), lambda b, h, i: (b * nq + i, h)),
            scratch_shapes=[pltpu.VMEM((seq, HEAD_DIM), BF16), pltpu.VMEM((seq, HEAD_DIM), BF16)],
        ),
        out_shape=jax.ShapeDtypeStruct((batch * seq, N_HEADS * HEAD_DIM), BF16),
        compiler_params=_cparams(("arbitrary", "arbitrary", "arbitrary")),
        name="sb_attention_prompt",
    )(bias, proj, proj, proj, _suffix_matrix())


def _sb_decode_kernel(pt_ref, qbd_ref, bias_ref, k_ref, v_ref, cumt_ref, expand_ref, o_ref,
                      run_scr, acc_scr, *, scale, n_pages):
    jj = pl.program_id(1)

    @pl.when(jj == 0)
    def _():
        run_scr[...] = jnp.zeros_like(run_scr)
        acc_scr[...] = jnp.zeros_like(acc_scr)

    page = k_ref.shape[1]
    width = k_ref.shape[2]
    kb = k_ref[0].astype(BF16)
    s = _dot(kb, qbd_ref[0]) * scale + bias_ref[...]
    t = _softplus_neg_abs(s)
    log_beta = jnp.minimum(s, 0.0) - t
    log_keep = -jnp.maximum(s, 0.0) - t
    hi, lo = _split2(log_keep)
    c = _dot(cumt_ref[...], hi) + _dot(cumt_ref[...], lo)
    w = jnp.exp(log_beta + (c[:page] + run_scr[...]))
    run_scr[...] = run_scr[...] + c[page:]
    wide = _dot(w.astype(BF16), expand_ref[...])
    prod = wide * v_ref[0]
    acc_scr[...] = acc_scr[...] + jnp.sum(prod.reshape(page // 8, 8, width), axis=0)

    @pl.when(jj == n_pages - 1)
    def _():
        o_ref[0] = jnp.sum(acc_scr[...], axis=0, keepdims=True)


def sb_attention_decode(q, bias, cache_k, cache_v, page_table):
    b, width = q.shape
    n_pages = page_table.shape[1]
    page = cache_k.shape[1]
    assert page == LANES
    head_of_row = jnp.arange(width) // HEAD_DIM
    qbd = jnp.where(head_of_row[None, :, None] == jnp.arange(LANES)[None, None, :], q[:, :, None], 0.0).astype(BF16)
    bias_row = jnp.zeros((1, LANES), F32).at[0, :N_HEADS].set(bias)
    r = lax.broadcasted_iota(I32, (2 * page, page), 0)
    cidx = lax.broadcasted_iota(I32, (2 * page, page), 1)
    cumt = jnp.where((r >= page) | (cidx > r), 1.0, 0.0).astype(BF16)
    lane = lax.broadcasted_iota(I32, (LANES, width), 0)
    col_head = lax.broadcasted_iota(I32, (LANES, width), 1) // HEAD_DIM
    expand = jnp.where(lane == col_head, 1.0, 0.0).astype(BF16)
    kern = functools.partial(_sb_decode_kernel, scale=HEAD_DIM ** -0.5, n_pages=n_pages)
    out = pl.pallas_call(
        kern,
        grid_spec=pltpu.PrefetchScalarGridSpec(
            num_scalar_prefetch=1,
            grid=(b, n_pages),
            in_specs=[
                pl.BlockSpec((1, width, LANES), lambda i, j, pt: (i, 0, 0)),
                pl.BlockSpec((1, LANES), lambda i, j, pt: (0, 0)),
                pl.BlockSpec((1, page, width), lambda i, j, pt: (pt[i, n_pages - 1 - j], 0, 0)),
                pl.BlockSpec((1, page, width), lambda i, j, pt: (pt[i, n_pages - 1 - j], 0, 0)),
                pl.BlockSpec((2 * page, page), lambda i, j, pt: (0, 0)),
                pl.BlockSpec((LANES, width), lambda i, j, pt: (0, 0)),
            ],
            out_specs=pl.BlockSpec((1, 1, width), lambda i, j, pt: (i, 0, 0)),
            scratch_shapes=[pltpu.VMEM((page, LANES), F32), pltpu.VMEM((8, width), F32)],
        ),
        out_shape=jax.ShapeDtypeStruct((b, 1, width), F32),
        compiler_params=_cparams(("arbitrary", "arbitrary")),
        name="sb_attention_decode",
    )(page_table, qbd, bias_row, cache_k, cache_v, cumt, expand)
    return out.reshape(b, width)


def _inv_unit_lower(n, size):
    rr = lax.broadcasted_iota(I32, n.shape, 0)
    cc = lax.broadcasted_iota(I32, n.shape, 1)
    t = jnp.where(rr == cc, 1.0, 0.0) - n
    p = n
    span = 2
    while span < size:
        p = _dot_wide(p, p)
        t = t + _dot_wide(t, p)
        span *= 2
    return t


def _gdn_kernel(xc_ref, z_ref, ba_ref, hist_ref, s0_ref, wconv_ref, arow_ref, dtrow_ref, gon_ref, lmat_ref,
                og_ref, sout_ref, xp_scr, s_scr, *, chunk, n_chunks, valid_rows):
    c = pl.program_id(1)
    qk_w = N_HEADS * HEAD_DIM

    @pl.when(c == 0)
    def _():
        xp_scr[5:8, :] = hist_ref[0]
        s_scr[...] = s0_ref[0]

    x = xc_ref[...]
    xp_scr[8:8 + chunk, :] = x
    wc = wconv_ref[...]
    conv = xp_scr[5:5 + chunk, :] * wc[0:1, :]
    conv = conv + xp_scr[6:6 + chunk, :] * wc[1:2, :]
    conv = conv + xp_scr[7:7 + chunk, :] * wc[2:3, :]
    conv = conv + x * wc[3:4, :]
    xp_scr[5:8, :] = x[chunk - 3:chunk, :]
    conv = conv * _sigmoid(conv)

    ba = ba_ref[...]
    beta_all = _sigmoid(ba)
    sp = ba + dtrow_ref[...]
    g_all = -arow_ref[...] * (jnp.maximum(sp, 0.0) + _softplus_neg_abs(sp))
    if valid_rows < chunk:
        live = lax.broadcasted_iota(I32, ba.shape, 0) < valid_rows
        beta_all = jnp.where(live, beta_all, 0.0)
        g_all = jnp.where(live, g_all, 0.0)
    stacked = _dot_exact_lhs(lmat_ref[...], g_all)
    gc_all = stacked[:chunk]
    gc_t = stacked[:LANES].T
    g_tot = stacked[LANES:]

    ii = lax.broadcasted_iota(I32, (chunk, chunk), 0)
    jj = lax.broadcasted_iota(I32, (chunk, chunk), 1)
    incl = ii >= jj
    strict = ii > jj
    gon = gon_ref[...]

    for h in range(N_HEADS):
        lo_c, hi_c = h * HEAD_DIM, (h + 1) * HEAD_DIM
        qh = conv[:, lo_c:hi_c]
        kh = conv[:, qk_w + lo_c:qk_w + hi_c]
        vh = conv[:, 2 * qk_w + lo_c:2 * qk_w + hi_c]
        qh = qh * lax.rsqrt(jnp.sum(qh * qh, axis=-1, keepdims=True) + EPS) * (HEAD_DIM ** -0.5)
        kh = kh * lax.rsqrt(jnp.sum(kh * kh, axis=-1, keepdims=True) + EPS)
        bcol = beta_all[:, h:h + 1]
        gcol = gc_all[:, N_HEADS + h:N_HEADS + h + 1]
        grow = gc_t[N_HEADS + h:N_HEADS + h + 1, :chunk]
        glast_c = g_tot[:chunk, N_HEADS + h:N_HEADS + h + 1]
        glast_s = g_tot[:, N_HEADS + h:N_HEADS + h + 1]
        decay = jnp.where(incl, jnp.exp(jnp.where(incl, gcol - grow, 0.0)), 0.0)
        kb = kh * bcol
        khb = kh.astype(BF16)
        n_mat = jnp.where(strict, _dot_nt(kb.astype(BF16), khb) * decay, 0.0)
        t_inv = _inv_unit_lower(n_mat, chunk)
        rhs = jnp.concatenate([vh * bcol, kb * jnp.exp(gcol)], axis=1)
        sol = _dot_wide(t_inv, rhs)
        u = sol[:, :HEAD_DIM]
        w = sol[:, HEAD_DIM:]
        qk = _dot_nt(qh.astype(BF16), khb) * decay
        q_dec = qh * jnp.exp(gcol)
        k_end = kh * jnp.exp(glast_c - gcol)
        state = s_scr[h]
        state_b = state.astype(BF16)
        v_new = u - _dot(w.astype(BF16), state_b)
        v_new_b = v_new.astype(BF16)
        o = _dot(q_dec.astype(BF16), state_b) + _dot(qk.astype(BF16), v_new_b)
        s_scr[h] = state * jnp.exp(glast_s) + _dot_tn(k_end.astype(BF16), v_new_b)
        zh = z_ref[:, lo_c:hi_c]
        og_ref[:, lo_c:hi_c] = (_rms(o, gon) * (zh * _sigmoid(zh))).astype(og_ref.dtype)

    @pl.when(c == n_chunks - 1)
    def _():
        sout_ref[0] = s_scr[...]


def gated_deltanet(xc, xc_col, zz, z_col, ba, hist, s0, w_conv, a_log, dt_bias, g_onorm, *,
                   batch, seq, chunk, valid_rows):
    n_chunks = seq // chunk
    width = N_HEADS * HEAD_DIM
    a_row = jnp.zeros((1, LANES), F32).at[0, N_HEADS:2 * N_HEADS].set(jnp.exp(a_log.astype(F32)))
    dt_row = jnp.zeros((1, LANES), F32).at[0, N_HEADS:2 * N_HEADS].set(dt_bias.astype(F32))
    r = lax.broadcasted_iota(I32, (2 * LANES, chunk), 0)
    t = lax.broadcasted_iota(I32, (2 * LANES, chunk), 1)
    lmat = jnp.where((r >= LANES) | (t <= r), 1.0, 0.0).astype(BF16)
    kern = functools.partial(_gdn_kernel, chunk=chunk, n_chunks=n_chunks, valid_rows=valid_rows)
    return pl.pallas_call(
        kern,
        grid=(batch, n_chunks),
        in_specs=[
            pl.BlockSpec((chunk, 3 * width), lambda b, c: (b * n_chunks + c, xc_col)),
            pl.BlockSpec((chunk, width), lambda b, c: (b * n_chunks + c, z_col)),
            pl.BlockSpec((chunk, LANES), lambda b, c: (b * n_chunks + c, 0)),
            pl.BlockSpec((1, CONV_TAPS - 1, 3 * width), lambda b, c: (b, 0, 0)),
            pl.BlockSpec((1, N_HEADS, HEAD_DIM, HEAD_DIM), lambda b, c: (b, 0, 0, 0)),
            pl.BlockSpec((CONV_TAPS, 3 * width), lambda b, c: (0, 0)),
            pl.BlockSpec((1, LANES), lambda b, c: (0, 0)),
            pl.BlockSpec((1, LANES), lambda b, c: (0, 0)),
            pl.BlockSpec((1, HEAD_DIM), lambda b, c: (0, 0)),
            pl.BlockSpec((2 * LANES, chunk), lambda b, c: (0, 0)),
        ],
        out_specs=[
            pl.BlockSpec((chunk, width), lambda b, c: (b * n_chunks + c, 0)),
            pl.BlockSpec((1, N_HEADS, HEAD_DIM, HEAD_DIM), lambda b, c: (b, 0, 0, 0)),
        ],
        out_shape=[
            jax.ShapeDtypeStruct((batch * seq, width), BF16),
            jax.ShapeDtypeStruct((batch, N_HEADS, HEAD_DIM, HEAD_DIM), F32),
        ],
        scratch_shapes=[pltpu.VMEM((8 + chunk, 3 * width), F32), pltpu.VMEM((N_HEADS, HEAD_DIM, HEAD_DIM), F32)],
        compiler_params=_cparams(("arbitrary", "arbitrary")),
        name="gated_deltanet",
    )(xc, zz, ba, hist, s0, w_conv, a_row, dt_row, g_onorm.reshape(1, HEAD_DIM), lmat)


def _merge_kernel(a_ref, b_ref, ga_ref, gb_ref, wa_ref, wb_ref, o_ref):
    ya = _dot(a_ref[...], wa_ref[...])
    yb = _dot(b_ref[...], wb_ref[...])
    o_ref[...] = (_sigmoid(ga_ref[...]) * ya + _sigmoid(gb_ref[...]) * yb).astype(o_ref.dtype)


def gated_merge(a, b, proj, ga_col, gb_col, wa, wb):
    m, k = a.shape
    n = wa.shape[1]
    tm, tn = MM_ROW_TILE, MM_COL_TILE
    return pl.pallas_call(
        _merge_kernel,
        grid=(n // tn, m // tm),
        in_specs=[
            pl.BlockSpec((tm, k), lambda j, i: (i, 0)),
            pl.BlockSpec((tm, k), lambda j, i: (i, 0)),
            pl.BlockSpec((tm, tn), lambda j, i: (i, ga_col + j)),
            pl.BlockSpec((tm, tn), lambda j, i: (i, gb_col + j)),
            pl.BlockSpec((k, tn), lambda j, i: (0, j)),
            pl.BlockSpec((k, tn), lambda j, i: (0, j)),
        ],
        out_specs=pl.BlockSpec((tm, tn), lambda j, i: (i, j)),
        out_shape=jax.ShapeDtypeStruct((m, n), BF16),
        compiler_params=_cparams(("arbitrary", "arbitrary")),
        name="gated_merge",
    )(a, b, proj, proj, wa, wb)


def _outproj_router_kernel(x_ref, mg_ref, wout_ref, g_ref, wrh_ref, wrl_ref, br_ref,
                           h_ref, u_ref, ti_ref, tw_ref, *, n_experts):
    h1 = x_ref[...] + _dot(mg_ref[...], wout_ref[...])
    h_ref[...] = h1
    u = _rms(h1, g_ref[...])
    u_ref[...] = u
    uh, ul = _split2(u)
    logits = _dot(uh, wrh_ref[...]) + (_dot(ul, wrh_ref[...]) + _dot(uh, wrl_ref[...])) + br_ref[...]
    lane = lax.broadcasted_iota(I32, logits.shape, 1).astype(F32)
    vals = jnp.where(lane < n_experts, logits, -jnp.inf)
    top_v, top_i = [], []
    for _ in range(TOP_K):
        m = jnp.max(vals, axis=-1, keepdims=True)
        idx = jnp.min(jnp.where(vals == m, lane, float(LANES)), axis=-1, keepdims=True)
        top_v.append(m)
        top_i.append(idx)
        vals = jnp.where(lane == idx, -jnp.inf, vals)
    ex = [jnp.exp(v - top_v[0]) for v in top_v]
    den = ex[0]
    for e in ex[1:]:
        den = den + e
    tw = jnp.zeros(logits.shape, F32)
    ti = jnp.zeros(logits.shape, F32)
    for k in range(TOP_K):
        tw = jnp.where(lane == k, ex[k] / den, tw)
        ti = jnp.where(lane == k, top_i[k], ti)
    tw_ref[...] = tw
    ti_ref[...] = ti.astype(I32)


def outproj_router(x, merged, w_out, g_ffn, w_router, b_router):
    m, d = x.shape
    n_experts = w_router.shape[1]
    tm = ROW_TILE
    wr = jnp.zeros((d, LANES), F32).at[:, :n_experts].set(w_router.astype(F32))
    wr_hi = wr.astype(BF16)
    wr_lo = (wr - wr_hi.astype(F32)).astype(BF16)
    br = jnp.zeros((1, LANES), F32).at[0, :n_experts].set(b_router.astype(F32))
    row = lambda i: (i, 0)
    fixed = lambda i: (0, 0)
    kern = functools.partial(_outproj_router_kernel, n_experts=n_experts)
    return pl.pallas_call(
        kern,
        grid=(m // tm,),
        in_specs=[
            pl.BlockSpec((tm, d), row), pl.BlockSpec((tm, d), row), pl.BlockSpec((d, d), fixed),
            pl.BlockSpec((1, d), fixed), pl.BlockSpec((d, LANES), fixed), pl.BlockSpec((d, LANES), fixed),
            pl.BlockSpec((1, LANES), fixed),
        ],
        out_specs=[pl.BlockSpec((tm, d), row), pl.BlockSpec((tm, d), row),
                   pl.BlockSpec((tm, LANES), row), pl.BlockSpec((tm, LANES), row)],
        out_shape=[jax.ShapeDtypeStruct((m, d), F32), jax.ShapeDtypeStruct((m, d), F32),
                   jax.ShapeDtypeStruct((m, LANES), I32), jax.ShapeDtypeStruct((m, LANES), F32)],
        compiler_params=_cparams(("arbitrary",)),
        name="outproj_router",
    )(x, merged, w_out, g_ffn.reshape(1, d), wr_hi, wr_lo, br)


def _row_copy(src_hbm, dst, src_row, dst_row, sem):
    return pltpu.make_async_copy(src_hbm.at[pl.ds(src_row, 1)], dst.at[pl.ds(dst_row, 1)], sem)


def _moe_kernel(te_ref, tr_ref, idx_ref, u_hbm, wg_ref, wl_ref, bg_ref, bl_ref, wd_ref, bd_ref,
                y_ref, xg_scr, xb_scr, wg_scr, wl_scr, wd_scr, sem, *, sub_rows):
    s = pl.program_id(0)
    f = pl.program_id(1)
    n_rows = tr_ref[s]

    @pl.when((s == 0) & (f == 0))
    def _():
        xg_scr[...] = jnp.zeros_like(xg_scr)

    @pl.when(n_rows > 0)
    def _():
        @pl.when(f == 0)
        def _():
            def issue(r, carry):
                _row_copy(u_hbm, xg_scr, idx_ref[0, 0, r], r, sem).start()
                return carry

            lax.fori_loop(0, n_rows, issue, 0)

            def drain(r, carry):
                _row_copy(u_hbm, xg_scr, 0, 0, sem).wait()
                return carry

            lax.fori_loop(0, n_rows, drain, 0)
            xb_scr[...] = xg_scr[...].astype(BF16)
            y_ref[...] = jnp.broadcast_to(bd_ref[0], y_ref.shape)

        wg_scr[...] = wg_ref[0].astype(BF16)
        wl_scr[...] = wl_ref[0].astype(BF16)
        wd_scr[...] = wd_ref[0].astype(BF16)
        bg = bg_ref[0]
        bl = bl_ref[0]

        def block(i, carry):
            r0 = pl.multiple_of(i * sub_rows, sub_rows)
            xb = xb_scr[pl.ds(r0, sub_rows), :]
            glu = jnp.minimum(_dot(xb, wg_scr[...]) + bg, SWIGLU_LIMIT)
            lin = jnp.clip(_dot(xb, wl_scr[...]) + bl, -SWIGLU_LIMIT, SWIGLU_LIMIT)
            act = glu * _sigmoid(SWIGLU_ALPHA * glu) * (lin + 1.0)
            y_ref[pl.ds(r0, sub_rows), :] += _dot(act.astype(BF16), wd_scr[...])
            return carry

        lax.fori_loop(0, (n_rows + sub_rows - 1) // sub_rows, block, 0)

    @pl.when((n_rows == 0) & (f == 0))
    def _():
        y_ref[...] = jnp.zeros_like(y_ref)


def routed_experts(u, tile_expert, tile_rows, src_tok, w_up, b_up, w_down, b_down):
    n_tiles, _, rows = src_tok.shape
    n_exp, d, two_ff = w_up.shape
    d_ff = two_ff // 2
    tf = MOE_FF_TILE
    nf = d_ff // tf

    def ff(s, f, tr):
        return jnp.where(tr[s] > 0, f, nf - 1)

    kern = functools.partial(_moe_kernel, sub_rows=MOE_SUB_ROWS)
    return pl.pallas_call(
        kern,
        grid_spec=pltpu.PrefetchScalarGridSpec(
            num_scalar_prefetch=2,
            grid=(n_tiles, nf),
            in_specs=[
                pl.BlockSpec((1, 1, rows), lambda s, f, te, tr: (s, 0, 0), memory_space=pltpu.SMEM),
                pl.BlockSpec(memory_space=pl.ANY),
                pl.BlockSpec((1, d, tf), lambda s, f, te, tr: (te[s], 0, ff(s, f, tr))),
                pl.BlockSpec((1, d, tf), lambda s, f, te, tr: (te[s], 0, nf + ff(s, f, tr))),
                pl.BlockSpec((1, 1, tf), lambda s, f, te, tr: (te[s], 0, ff(s, f, tr))),
                pl.BlockSpec((1, 1, tf), lambda s, f, te, tr: (te[s], 0, nf + ff(s, f, tr))),
                pl.BlockSpec((1, tf, d), lambda s, f, te, tr: (te[s], ff(s, f, tr), 0)),
                pl.BlockSpec((1, 1, d), lambda s, f, te, tr: (te[s], 0, 0)),
            ],
            out_specs=pl.BlockSpec((rows, d), lambda s, f, te, tr: (s, 0)),
            scratch_shapes=[
                pltpu.VMEM((rows, d), F32), pltpu.VMEM((rows, d), BF16),
                pltpu.VMEM((d, tf), BF16), pltpu.VMEM((d, tf), BF16), pltpu.VMEM((tf, d), BF16),
                pltpu.SemaphoreType.DMA(()),
            ],
        ),
        out_shape=jax.ShapeDtypeStruct((n_tiles * rows, d), F32),
        compiler_params=_cparams(("arbitrary", "arbitrary")),
        name="routed_experts",
    )(tile_expert, tile_rows, src_tok, u, w_up, w_up,
      b_up.reshape(n_exp, 1, two_ff), b_up.reshape(n_exp, 1, two_ff), w_down, b_down.reshape(n_exp, 1, d))


def routing_tables(top_idx, n_tokens, n_experts, rows):
    n_assign = n_tokens * TOP_K
    n_tiles = n_assign // rows + n_experts
    e_flat = top_idx[:n_tokens, :TOP_K].reshape(-1)
    onehot = (e_flat[:, None] == jnp.arange(n_experts, dtype=I32)[None, :]).astype(I32)
    csum = jnp.cumsum(onehot, axis=0)
    count = csum[-1]
    pos = jnp.sum(csum * onehot, axis=1) - 1
    tiles_of = (count + rows - 1) // rows
    tile_end = jnp.cumsum(tiles_of)
    tile_start = tile_end - tiles_of
    used = tile_end[-1]
    dest = tile_start[e_flat] * rows + pos
    tile_id = jnp.arange(n_tiles, dtype=I32)
    last = used - 1
    tile_block = jnp.minimum(tile_id, last)
    tile_expert = jnp.sum((tile_end[None, :] <= tile_block[:, None]).astype(I32), axis=1)
    tile_rows = jnp.clip(count[tile_expert] - (tile_id - tile_start[tile_expert]) * rows, 0, rows)
    tile_rows = jnp.where(tile_id < used, tile_rows, 0)
    token_of = jnp.arange(n_assign, dtype=I32) // TOP_K
    src_tok = jnp.zeros((n_tiles * rows,), I32).at[dest].set(token_of).reshape(n_tiles, 1, rows)
    return tile_expert.astype(I32), tile_rows.astype(I32), src_tok, dest.astype(I32)


def _final_kernel(dst_ref, y_hbm, h_ref, tw_ref, p_ref, gple_ref, wgate_ref, wproj_ref, gfin_ref, o_ref,
                  buf, sem, *, tm):
    def issue(r, carry):
        for k in range(TOP_K):
            _row_copy(y_hbm, buf.at[k], dst_ref[0, 0, k * tm + r], r, sem).start()
        return carry

    lax.fori_loop(0, tm, issue, 0)

    def drain(r, carry):
        for k in range(TOP_K):
            _row_copy(y_hbm, buf.at[k], 0, 0, sem).wait()
        return carry

    lax.fori_loop(0, tm, drain, 0)
    tw = tw_ref[...]
    h2 = h_ref[...]
    for k in range(TOP_K):
        h2 = h2 + tw[:, k:k + 1] * buf[k]
    un = _rms(h2, gple_ref[...]).astype(BF16)
    gate = _sigmoid(_dot(un, wgate_ref[...]))
    h3 = h2 + gate * _dot(p_ref[...].astype(BF16), wproj_ref[...])
    o_ref[...] = _rms(h3, gfin_ref[...])


def combine_ple_final(y_rows, dest, h1, top_w, p, g_ple, w_gate, w_proj, g_final):
    m, d = h1.shape
    tm = ROW_TILE
    pd = p.shape[1]
    row = lambda i: (i, 0)
    fixed = lambda i: (0, 0)
    kern = functools.partial(_final_kernel, tm=tm)
    return pl.pallas_call(
        kern,
        grid=(m // tm,),
        in_specs=[
            pl.BlockSpec((1, 1, TOP_K * tm), lambda i: (i, 0, 0), memory_space=pltpu.SMEM),
            pl.BlockSpec(memory_space=pl.ANY),
            pl.BlockSpec((tm, d), row), pl.BlockSpec((tm, LANES), row), pl.BlockSpec((tm, pd), row),
            pl.BlockSpec((1, d), fixed), pl.BlockSpec((d, d), fixed), pl.BlockSpec((pd, d), fixed),
            pl.BlockSpec((1, d), fixed),
        ],
        out_specs=pl.BlockSpec((tm, d), row),
        out_shape=jax.ShapeDtypeStruct((m, d), F32),
        scratch_shapes=[pltpu.VMEM((TOP_K, tm, d), F32), pltpu.SemaphoreType.DMA(())],
        compiler_params=_cparams(("arbitrary",)),
        name="combine_ple_final",
    )(dest, y_rows, h1, top_w, p, g_ple.reshape(1, d), w_gate, w_proj, g_final.reshape(1, d))


def kernel(x_prompt, x_sample, cache_k, cache_v, page_table, state_conv, state_ssm, p_prompt, p_sample, g_mix, w_in, b_sb, w_conv, a_log, dt_bias, g_onorm, w_o_sb, w_o_gdn, w_out, g_ffn, w_router, b_router, w_up, b_up, w_down, b_down, g_ple, w_ple_gate, w_ple_proj, g_final):
    bp, sp, d = x_prompt.shape
    bs = x_sample.shape[0]
    assert x_sample.shape[1] == 1 and g_mix.shape[0] == 1
    n_prompt = bp * sp
    n_tok = n_prompt + bs
    tp = -(-n_tok // MM_ROW_TILE) * MM_ROW_TILE
    width = N_HEADS * HEAD_DIM
    conv_w = 3 * width
    n_experts = w_router.shape[2]

    ba_lo = 3 * width + conv_w + width
    ba_hi = ba_lo + 2 * N_HEADS
    w_main = jnp.concatenate([w_in[0][:, :ba_lo], w_in[0][:, ba_hi:]], axis=1).astype(BF16)
    w_ba = jnp.zeros((d, LANES), BF16).at[:, :2 * N_HEADS].set(w_in[0][:, ba_lo:ba_hi].astype(BF16))

    x_all = jnp.concatenate([x_prompt.reshape(n_prompt, d), x_sample.reshape(bs, d),
                             jnp.zeros((tp - n_tok, d), x_prompt.dtype)], axis=0)
    xn = rmsnorm_cast(x_all, g_mix[0], ROW_TILE)
    proj = matmul(xn, w_main, MM_ROW_TILE, MM_COL_TILE, "in_proj")
    ba = matmul(xn, w_ba, MM_ROW_TILE, LANES, "in_proj_gates")

    heads = lambda t, b, s: t.reshape(1, b, s, N_HEADS, HEAD_DIM)
    k_prompt = heads(proj[:n_prompt, width:2 * width], bp, sp)
    v_prompt = heads(proj[:n_prompt, 2 * width:3 * width], bp, sp)
    k_sample = heads(proj[n_prompt:n_tok, width:2 * width], bs, 1)
    v_sample = heads(proj[n_prompt:n_tok, 2 * width:3 * width], bs, 1)

    blk = width // HEAD_DIM
    o_sb_p = sb_attention_prompt(proj, b_sb[0].astype(F32), bp, sp, 0, blk, 2 * blk)
    n_pool, page = cache_k.shape[1], cache_k.shape[2]
    o_sb_s = sb_attention_decode(proj[n_prompt:n_tok, :width], b_sb[0].astype(F32),
                                 cache_k[0].reshape(n_pool, page, width), cache_v[0].reshape(n_pool, page, width),
                                 page_table)
    o_sb = jnp.concatenate([o_sb_p, o_sb_s.astype(BF16), jnp.zeros((tp - n_tok, width), BF16)], axis=0)

    conv_blk = 3 * width // conv_w
    z_blk = (3 * width + conv_w) // width
    og_p, ssm_p = gated_deltanet(
        proj, conv_blk, proj, z_blk, ba, jnp.zeros((bp, CONV_TAPS - 1, conv_w), F32),
        jnp.zeros((bp, N_HEADS, HEAD_DIM, HEAD_DIM), F32), w_conv[0], a_log[0], dt_bias[0], g_onorm[0],
        batch=bp, seq=sp, chunk=GDN_CHUNK, valid_rows=GDN_CHUNK)
    pad_rows = GDN_CHUNK
    spread = lambda t: jnp.zeros((bs, pad_rows, t.shape[1]), F32).at[:, 0].set(t).reshape(bs * pad_rows, t.shape[1])
    conv_in_s = proj[n_prompt:n_tok, 3 * width:3 * width + conv_w]
    og_s, ssm_s = gated_deltanet(
        spread(conv_in_s), 0, spread(proj[n_prompt:n_tok, 3 * width + conv_w:ba_lo]), 0, spread(ba[n_prompt:n_tok]),
        state_conv[0].astype(F32), state_ssm[0].astype(F32), w_conv[0], a_log[0], dt_bias[0], g_onorm[0],
        batch=bs, seq=pad_rows, chunk=pad_rows, valid_rows=1)
    o_g = jnp.concatenate([og_p, og_s.reshape(bs, pad_rows, width)[:, 0], jnp.zeros((tp - n_tok, width), BF16)], axis=0)
    conv_prompt = proj[:n_prompt, 3 * width:3 * width + conv_w].reshape(bp, sp, conv_w)[:, sp - (CONV_TAPS - 1):][None]
    conv_sample = jnp.concatenate([state_conv[0][:, 1:].astype(F32), conv_in_s[:, None]], axis=1)[None]

    gate_blk = ba_lo // MM_COL_TILE
    merged = gated_merge(o_sb, o_g, proj, gate_blk, gate_blk + d // MM_COL_TILE,
                         w_o_sb[0].astype(BF16), w_o_gdn[0].astype(BF16))
    h1, u2, top_i, top_w = outproj_router(x_all, merged, w_out[0].astype(BF16), g_ffn[0], w_router[0], b_router[0])

    te, tr, src_tok, dest = routing_tables(top_i, n_tok, n_experts, MOE_TILE_ROWS)
    y_rows = routed_experts(u2, te, tr, src_tok, w_up[0], b_up[0], w_down[0], b_down[0])

    dest_tok = jnp.zeros((tp, TOP_K), I32).at[:n_tok].set(dest.reshape(n_tok, TOP_K))
    dest_tiles = dest_tok.reshape(tp // ROW_TILE, ROW_TILE, TOP_K).transpose(0, 2, 1).reshape(tp // ROW_TILE, 1, TOP_K * ROW_TILE)
    p_all = jnp.concatenate([p_prompt[0].reshape(n_prompt, -1), p_sample[0].reshape(bs, -1),
                             jnp.zeros((tp - n_tok, p_prompt.shape[-1]), p_prompt.dtype)], axis=0)
    y = combine_ple_final(y_rows, dest_tiles, h1, top_w, p_all, g_ple[0], w_ple_gate[0].astype(BF16),
                          w_ple_proj[0].astype(BF16), g_final)

    y_prompt = y[:n_prompt].reshape(bp, sp, d)
    y_sample = y[n_prompt:n_tok].reshape(bs, 1, d)
    return (y_prompt, y_sample, k_prompt, v_prompt, k_sample, v_sample,
            conv_prompt, ssm_p[None], conv_sample, ssm_s[None])
```

```python
import functools

import jax
import jax.numpy as jnp
from jax import lax
from jax.experimental import pallas as pl
from jax.experimental.pallas import tpu as pltpu

F32 = jnp.float32
BF16 = jnp.bfloat16
I32 = jnp.int32

EPS = 1e-6
LANES = 128
HEAD_DIM = 128
N_HEADS = 8
CONV_TAPS = 4
GDN_CHUNK = 64
GDN_DECODE_ROWS = 16
TOP_K = 4
SWIGLU_ALPHA = 1.702
SWIGLU_LIMIT = 7.0
VMEM_LIMIT = 56 * 1024 * 1024

ROW_TILE = 256
MM_ROW_TILE = 512
MM_COL_TILE = 1024
ATT_TILE = 256
ATT_HEAD_GROUP = 4
DECODE_PAGE_GROUP = 8
MOE_TILE_ROWS = 1280
MOE_SUB_ROWS = 256
MOE_FF_TILE = 256
DMA_UNROLL = 8


def _cparams(sem):
    return pltpu.CompilerParams(dimension_semantics=sem, vmem_limit_bytes=VMEM_LIMIT)


def _sigmoid(x):
    return 1.0 / (1.0 + jnp.exp(-x))


def _softplus_neg_abs(x):
    return jnp.log(1.0 + jnp.exp(-jnp.abs(x)))


def _split2(x):
    hi = x.astype(BF16)
    lo = (x - hi.astype(F32)).astype(BF16)
    return hi, lo


def _split3(x):
    hi = x.astype(BF16)
    r = x - hi.astype(F32)
    mid = r.astype(BF16)
    lo = (r - mid.astype(F32)).astype(BF16)
    return hi, mid, lo


def _dot(a, b):
    return jnp.dot(a, b, preferred_element_type=F32)


def _dot_nt(a, b):
    return lax.dot_general(a, b, (((1,), (1,)), ((), ())), preferred_element_type=F32)


def _dot_tn(a, b):
    return lax.dot_general(a, b, (((0,), (0,)), ((), ())), preferred_element_type=F32)


def _dot_wide(a, b):
    ah, al = _split2(a)
    bh, bl = _split2(b)
    return _dot(ah, bh) + (_dot(ah, bl) + _dot(al, bh))


def _dot_exact_lhs(a_bf16, x):
    hi, mid, lo = _split3(x)
    return _dot(a_bf16, hi) + (_dot(a_bf16, mid) + _dot(a_bf16, lo))


def _rms(x, g):
    return x * lax.rsqrt(jnp.mean(x * x, axis=-1, keepdims=True) + EPS) * g


def _rms_kernel(x_ref, g_ref, o_ref):
    o_ref[...] = _rms(x_ref[...], g_ref[...]).astype(o_ref.dtype)


def rmsnorm_cast(x, g, tm):
    m, d = x.shape
    return pl.pallas_call(
        _rms_kernel,
        grid=(m // tm,),
        in_specs=[pl.BlockSpec((tm, d), lambda i: (i, 0)), pl.BlockSpec((1, d), lambda i: (0, 0))],
        out_specs=pl.BlockSpec((tm, d), lambda i: (i, 0)),
        out_shape=jax.ShapeDtypeStruct((m, d), BF16),
        compiler_params=_cparams(("arbitrary",)),
        name="rmsnorm_cast",
    )(x, g.reshape(1, d))


def _mm_kernel(a_ref, w_ref, o_ref):
    o_ref[...] = _dot(a_ref[...], w_ref[...]).astype(o_ref.dtype)


def matmul(a, w, tm, tn, name):
    m, k = a.shape
    n = w.shape[1]
    return pl.pallas_call(
        _mm_kernel,
        grid=(n // tn, m // tm),
        in_specs=[pl.BlockSpec((tm, k), lambda j, i: (i, 0)), pl.BlockSpec((k, tn), lambda j, i: (0, j))],
        out_specs=pl.BlockSpec((tm, tn), lambda j, i: (i, j)),
        out_shape=jax.ShapeDtypeStruct((m, n), F32),
        compiler_params=_cparams(("arbitrary", "arbitrary")),
        name=name,
    )(a, w)


def _suffix_matrix():
    j = lax.broadcasted_iota(I32, (LANES, 2 * LANES), 0)
    s = lax.broadcasted_iota(I32, (LANES, 2 * LANES), 1)
    return jnp.where((s >= LANES) | (j > s), 1.0, 0.0).astype(BF16)


def _sb_prompt_kernel(bias_ref, q_ref, k_ref, v_ref, cum_ref, o_ref, *, tile, scale, group):
    hg = pl.program_id(1)
    i = pl.program_id(2)
    heads = range(group)
    col = lambda h: slice(h * HEAD_DIM, (h + 1) * HEAD_DIM)
    q = [q_ref[:, col(h)] for h in heads]
    bias = [bias_ref[hg * group + h] for h in heads]
    cum = cum_ref[...]
    rows = lax.broadcasted_iota(I32, (tile, LANES), 0)
    cols = lax.broadcasted_iota(I32, (tile, LANES), 1)
    nsub = tile // LANES

    def block(j, carry, diagonal):
        run, acc = carry
        k0 = pl.multiple_of(j * tile, tile)
        s = [_dot_nt(q[h], k_ref[pl.ds(k0, tile), col(h)]) * scale + bias[h] for h in heads]
        for sb in reversed(range(nsub)):
            sub = [s[h][:, sb * LANES:(sb + 1) * LANES] for h in heads]
            t = [_softplus_neg_abs(x) for x in sub]
            log_beta = [jnp.minimum(x, 0.0) - tt for x, tt in zip(sub, t)]
            log_keep = [-jnp.maximum(x, 0.0) - tt for x, tt in zip(sub, t)]
            if diagonal:
                vis = (cols + sb * LANES) < rows
                log_keep = [jnp.where(vis, x, 0.0) for x in log_keep]
            parts = [_split2(x) for x in log_keep]
            c = [_dot(hi, cum) + _dot(lo, cum) for hi, lo in parts]
            w = [jnp.exp(log_beta[h] + (c[h][:, :LANES] + run[h])) for h in heads]
            if diagonal:
                w = [jnp.where(vis, x, 0.0) for x in w]
            run = [run[h] + c[h][:, LANES:] for h in heads]
            ks = pl.multiple_of(k0 + sb * LANES, LANES)
            acc = [acc[h] + _dot(w[h].astype(BF16), v_ref[pl.ds(ks, LANES), col(h)]) for h in heads]
        return run, acc

    zero = [jnp.zeros((tile, LANES), F32) for _ in heads]
    carry = block(i, (zero, zero), True)
    _, acc = lax.fori_loop(1, i + 1, lambda jj, c: block(i - jj, c, False), carry)
    for h in heads:
        o_ref[:, col(h)] = acc[h].astype(o_ref.dtype)


def sb_attention_prompt(qkv, bias, batch, seq):
    tile = ATT_TILE
    group = ATT_HEAD_GROUP
    nq = seq // tile
    ng = N_HEADS // group
    gw = group * HEAD_DIM
    kern = functools.partial(_sb_prompt_kernel, tile=tile, scale=HEAD_DIM ** -0.5, group=group)
    return pl.pallas_call(
        kern,
        grid=(batch, ng, nq),
        in_specs=[
            pl.BlockSpec(memory_space=pltpu.SMEM),
            pl.BlockSpec((tile, gw), lambda b, g, i: (b * nq + i, g)),
            pl.BlockSpec((seq, gw), lambda b, g, i: (b, ng + g)),
            pl.BlockSpec((seq, gw), lambda b, g, i: (b, 2 * ng + g)),
            pl.BlockSpec((LANES, 2 * LANES), lambda b, g, i: (0, 0)),
        ],
        out_specs=pl.BlockSpec((tile, gw), lambda b, g, i: (b * nq + i, g)),
        out_shape=jax.ShapeDtypeStruct((batch * seq, N_HEADS * HEAD_DIM), BF16),
        compiler_params=_cparams(("arbitrary", "arbitrary", "arbitrary")),
        name="sb_attention_prompt",
    )(bias, qkv, qkv, qkv, _suffix_matrix())


def _sb_decode_kernel(pt_ref, qt_ref, bias_ref, *refs, scale, n_steps, group, page):
    k_refs, v_refs = refs[:group], refs[group:2 * group]
    cumt_ref, expand_ref, o_ref, z_scr, run_scr, acc_scr = refs[2 * group:]
    jj = pl.program_id(1)

    @pl.when(jj == 0)
    def _():
        run_scr[...] = jnp.zeros_like(run_scr)
        acc_scr[...] = jnp.zeros_like(acc_scr)

    pages = range(group)
    lane = lax.broadcasted_iota(I32, (page, LANES), 1)
    qt = qt_ref[0]
    bias = bias_ref[...]
    for g in pages:
        z_scr[g] = _dot(k_refs[g][0].astype(BF16), qt)
    s = []
    for g in pages:
        acc = jnp.zeros((page, LANES), F32)
        for h in range(N_HEADS):
            acc = jnp.where(lane == h, z_scr[g, pl.ds(h, page, stride=N_HEADS), :], acc)
        s.append(acc * scale + bias)
    t = [_softplus_neg_abs(x) for x in s]
    log_beta = [jnp.minimum(x, 0.0) - tt for x, tt in zip(s, t)]
    log_keep = [-jnp.maximum(x, 0.0) - tt for x, tt in zip(s, t)]
    parts = [_split2(x) for x in log_keep]
    c = [_dot(cumt_ref[...], hi) + _dot(cumt_ref[...], lo) for hi, lo in parts]
    run = run_scr[...]
    w = []
    for g in pages:
        w.append(jnp.exp(log_beta[g] + (c[g][:page] + run)))
        run = run + c[g][page:]
    run_scr[...] = run
    wide = [_dot(x.astype(BF16), expand_ref[...]) for x in w]
    for h in range(N_HEADS):
        tot = acc_scr[h]
        for g in pages:
            vh = v_refs[g][0, pl.ds(h, page, stride=N_HEADS), :]
            prod = wide[g][:, h * HEAD_DIM:(h + 1) * HEAD_DIM] * vh
            tot = tot + jnp.sum(prod.reshape(page // 8, 8, HEAD_DIM), axis=0)
        acc_scr[h] = tot

    @pl.when(jj == n_steps - 1)
    def _():
        o_ref[0] = jnp.concatenate([jnp.sum(acc_scr[h], axis=0, keepdims=True) for h in range(N_HEADS)], axis=0)


def sb_attention_decode(q, bias, cache_k, cache_v, page_table):
    b, width = q.shape
    n_pages = page_table.shape[1]
    page = cache_k.shape[1] // N_HEADS
    group = DECODE_PAGE_GROUP
    assert page == LANES and n_pages % group == 0 and cache_k.shape[2] == HEAD_DIM
    n_steps = n_pages // group
    qt = jnp.zeros((b, HEAD_DIM, LANES), F32).at[:, :, :N_HEADS].set(
        q.reshape(b, N_HEADS, HEAD_DIM).transpose(0, 2, 1)).astype(BF16)
    bias_row = jnp.zeros((1, LANES), F32).at[0, :N_HEADS].set(bias)
    r = lax.broadcasted_iota(I32, (2 * page, page), 0)
    cidx = lax.broadcasted_iota(I32, (2 * page, page), 1)
    cumt = jnp.where((r >= page) | (cidx > r), 1.0, 0.0).astype(BF16)
    lane = lax.broadcasted_iota(I32, (LANES, width), 0)
    col_head = lax.broadcasted_iota(I32, (LANES, width), 1) // HEAD_DIM
    expand = jnp.where(lane == col_head, 1.0, 0.0).astype(BF16)

    def page_spec(g):
        return pl.BlockSpec((1, page * N_HEADS, HEAD_DIM),
                            lambda i, j, pt: (pt[i, n_pages - 1 - (j * group + g)], 0, 0))

    kern = functools.partial(_sb_decode_kernel, scale=HEAD_DIM ** -0.5, n_steps=n_steps, group=group, page=page)
    out = pl.pallas_call(
        kern,
        grid_spec=pltpu.PrefetchScalarGridSpec(
            num_scalar_prefetch=1,
            grid=(b, n_steps),
            in_specs=[pl.BlockSpec((1, HEAD_DIM, LANES), lambda i, j, pt: (i, 0, 0)),
                      pl.BlockSpec((1, LANES), lambda i, j, pt: (0, 0))]
                     + [page_spec(g) for g in range(group)] * 2
                     + [pl.BlockSpec((2 * page, page), lambda i, j, pt: (0, 0)),
                        pl.BlockSpec((LANES, width), lambda i, j, pt: (0, 0))],
            out_specs=pl.BlockSpec((1, N_HEADS, HEAD_DIM), lambda i, j, pt: (i, 0, 0)),
            scratch_shapes=[pltpu.VMEM((group, page * N_HEADS, LANES), F32), pltpu.VMEM((page, LANES), F32),
                            pltpu.VMEM((N_HEADS, 8, HEAD_DIM), F32)],
        ),
        out_shape=jax.ShapeDtypeStruct((b, N_HEADS, HEAD_DIM), F32),
        compiler_params=_cparams(("arbitrary", "arbitrary")),
        name="sb_attention_decode",
    )(page_table, qt, bias_row, *([cache_k] * group), *([cache_v] * group), cumt, expand)
    return out.reshape(b, width)


def _inv_unit_lower(mats, size):
    rr = lax.broadcasted_iota(I32, mats[0].shape, 0)
    cc = lax.broadcasted_iota(I32, mats[0].shape, 1)
    eye = jnp.where(rr == cc, 1.0, 0.0)
    t = [eye - n for n in mats]
    p = list(mats)
    span = 2
    while span < size:
        p = [_dot_wide(x, x) for x in p]
        t = [x + _dot_wide(x, y) for x, y in zip(t, p)]
        span *= 2
    return t


def _gdn_kernel(xc_ref, z_ref, ba_ref, hist_ref, s0_ref, wconv_ref, arow_ref, dtrow_ref, gon_ref, lmat_ref,
                og_ref, sout_ref, xp_scr, s_scr, *, chunk, n_chunks, valid_rows):
    c = pl.program_id(1)
    qk_w = N_HEADS * HEAD_DIM

    @pl.when(c == 0)
    def _():
        xp_scr[5:8, :] = hist_ref[0]
        s_scr[...] = s0_ref[0]

    x = xc_ref[...]
    xp_scr[8:8 + chunk, :] = x
    wc = wconv_ref[...]
    conv = xp_scr[5:5 + chunk, :] * wc[0:1, :]
    conv = conv + xp_scr[6:6 + chunk, :] * wc[1:2, :]
    conv = conv + xp_scr[7:7 + chunk, :] * wc[2:3, :]
    conv = conv + x * wc[3:4, :]
    xp_scr[5:8, :] = x[chunk - 3:chunk, :]
    conv = conv * _sigmoid(conv)

    ba = ba_ref[...]
    beta_all = _sigmoid(ba)
    sp = ba + dtrow_ref[...]
    g_all = -arow_ref[...] * (jnp.maximum(sp, 0.0) + _softplus_neg_abs(sp))
    if valid_rows < chunk:
        live = lax.broadcasted_iota(I32, ba.shape, 0) < valid_rows
        beta_all = jnp.where(live, beta_all, 0.0)
        g_all = jnp.where(live, g_all, 0.0)
    stacked = _dot_exact_lhs(lmat_ref[...], g_all)
    gc_all = stacked[:chunk]
    gc_t = stacked[:LANES].T
    g_tot = stacked[LANES:]

    ii = lax.broadcasted_iota(I32, (chunk, chunk), 0)
    jj = lax.broadcasted_iota(I32, (chunk, chunk), 1)
    incl = ii >= jj
    strict = ii > jj
    gon = gon_ref[...]

    heads = range(N_HEADS)
    col = lambda h, part: slice(part * qk_w + h * HEAD_DIM, part * qk_w + (h + 1) * HEAD_DIM)
    l2n = lambda x: x * lax.rsqrt(jnp.sum(x * x, axis=-1, keepdims=True) + EPS)
    q = [l2n(conv[:, col(h, 0)]) * (HEAD_DIM ** -0.5) for h in heads]
    k = [l2n(conv[:, col(h, 1)]) for h in heads]
    v = [conv[:, col(h, 2)] for h in heads]
    gate_lane = lambda h: slice(N_HEADS + h, N_HEADS + h + 1)
    bcol = [beta_all[:, h:h + 1] for h in heads]
    gcol = [gc_all[:, gate_lane(h)] for h in heads]
    grow = [gc_t[gate_lane(h), :chunk] for h in heads]
    glast_c = [g_tot[:chunk, gate_lane(h)] for h in heads]
    glast_s = [g_tot[:, gate_lane(h)] for h in heads]
    decay = [jnp.where(incl, jnp.exp(jnp.where(incl, gcol[h] - grow[h], 0.0)), 0.0) for h in heads]
    kb = [k[h] * bcol[h] for h in heads]
    k_b16 = [x.astype(BF16) for x in k]
    n_mat = [jnp.where(strict, _dot_nt(kb[h].astype(BF16), k_b16[h]) * decay[h], 0.0) for h in heads]
    t_inv = _inv_unit_lower(n_mat, chunk)
    rhs = [jnp.concatenate([v[h] * bcol[h], kb[h] * jnp.exp(gcol[h])], axis=1) for h in heads]
    sol = [_dot_wide(t_inv[h], rhs[h]) for h in heads]
    qk = [_dot_nt(q[h].astype(BF16), k_b16[h]) * decay[h] for h in heads]
    q_dec = [(q[h] * jnp.exp(gcol[h])).astype(BF16) for h in heads]
    k_end = [(k[h] * jnp.exp(glast_c[h] - gcol[h])).astype(BF16) for h in heads]
    state = [s_scr[h] for h in heads]
    state_b = [x.astype(BF16) for x in state]
    v_new = [sol[h][:, :HEAD_DIM] - _dot(sol[h][:, HEAD_DIM:].astype(BF16), state_b[h]) for h in heads]
    v_new_b = [x.astype(BF16) for x in v_new]
    o = [_dot(q_dec[h], state_b[h]) + _dot(qk[h].astype(BF16), v_new_b[h]) for h in heads]
    new_state = [state[h] * jnp.exp(glast_s[h]) + _dot_tn(k_end[h], v_new_b[h]) for h in heads]
    for h in heads:
        s_scr[h] = new_state[h]
    for h in heads:
        zh = z_ref[:, col(h, 0)]
        og_ref[:, col(h, 0)] = (_rms(o[h], gon) * (zh * _sigmoid(zh))).astype(og_ref.dtype)

    @pl.when(c == n_chunks - 1)
    def _():
        sout_ref[0] = s_scr[...]


def gated_deltanet(xc, xc_col, zz, z_col, ba, hist, s0, w_conv, a_log, dt_bias, g_onorm, *,
                   batch, seq, chunk, valid_rows):
    n_chunks = seq // chunk
    width = N_HEADS * HEAD_DIM
    a_row = jnp.zeros((1, LANES), F32).at[0, N_HEADS:2 * N_HEADS].set(jnp.exp(a_log.astype(F32)))
    dt_row = jnp.zeros((1, LANES), F32).at[0, N_HEADS:2 * N_HEADS].set(dt_bias.astype(F32))
    r = lax.broadcasted_iota(I32, (2 * LANES, chunk), 0)
    t = lax.broadcasted_iota(I32, (2 * LANES, chunk), 1)
    lmat = jnp.where((r >= LANES) | (t <= r), 1.0, 0.0).astype(BF16)
    kern = functools.partial(_gdn_kernel, chunk=chunk, n_chunks=n_chunks, valid_rows=valid_rows)
    return pl.pallas_call(
        kern,
        grid=(batch, n_chunks),
        in_specs=[
            pl.BlockSpec((chunk, 3 * width), lambda b, c: (b * n_chunks + c, xc_col)),
            pl.BlockSpec((chunk, width), lambda b, c: (b * n_chunks + c, z_col)),
            pl.BlockSpec((chunk, LANES), lambda b, c: (b * n_chunks + c, 0)),
            pl.BlockSpec((1, CONV_TAPS - 1, 3 * width), lambda b, c: (b, 0, 0)),
            pl.BlockSpec((1, N_HEADS, HEAD_DIM, HEAD_DIM), lambda b, c: (b, 0, 0, 0)),
            pl.BlockSpec((CONV_TAPS, 3 * width), lambda b, c: (0, 0)),
            pl.BlockSpec((1, LANES), lambda b, c: (0, 0)),
            pl.BlockSpec((1, LANES), lambda b, c: (0, 0)),
            pl.BlockSpec((1, HEAD_DIM), lambda b, c: (0, 0)),
            pl.BlockSpec((2 * LANES, chunk), lambda b, c: (0, 0)),
        ],
        out_specs=[
            pl.BlockSpec((chunk, width), lambda b, c: (b * n_chunks + c, 0)),
            pl.BlockSpec((1, N_HEADS, HEAD_DIM, HEAD_DIM), lambda b, c: (b, 0, 0, 0)),
        ],
        out_shape=[
            jax.ShapeDtypeStruct((batch * seq, width), BF16),
            jax.ShapeDtypeStruct((batch, N_HEADS, HEAD_DIM, HEAD_DIM), F32),
        ],
        scratch_shapes=[pltpu.VMEM((8 + chunk, 3 * width), F32), pltpu.VMEM((N_HEADS, HEAD_DIM, HEAD_DIM), F32)],
        compiler_params=_cparams(("arbitrary", "arbitrary")),
        name="gated_deltanet",
    )(xc, zz, ba, hist, s0, w_conv, a_row, dt_row, g_onorm.reshape(1, HEAD_DIM), lmat)


def _merge_kernel(a_ref, b_ref, ga_ref, gb_ref, wa_ref, wb_ref, o_ref):
    ya = _dot(a_ref[...], wa_ref[...])
    yb = _dot(b_ref[...], wb_ref[...])
    o_ref[...] = (_sigmoid(ga_ref[...]) * ya + _sigmoid(gb_ref[...]) * yb).astype(o_ref.dtype)


def gated_merge(a, b, proj, ga_col, gb_col, wa, wb):
    m, k = a.shape
    n = wa.shape[1]
    tm, tn = MM_ROW_TILE, MM_COL_TILE
    return pl.pallas_call(
        _merge_kernel,
        grid=(n // tn, m // tm),
        in_specs=[
            pl.BlockSpec((tm, k), lambda j, i: (i, 0)),
            pl.BlockSpec((tm, k), lambda j, i: (i, 0)),
            pl.BlockSpec((tm, tn), lambda j, i: (i, ga_col + j)),
            pl.BlockSpec((tm, tn), lambda j, i: (i, gb_col + j)),
            pl.BlockSpec((k, tn), lambda j, i: (0, j)),
            pl.BlockSpec((k, tn), lambda j, i: (0, j)),
        ],
        out_specs=pl.BlockSpec((tm, tn), lambda j, i: (i, j)),
        out_shape=jax.ShapeDtypeStruct((m, n), BF16),
        compiler_params=_cparams(("arbitrary", "arbitrary")),
        name="gated_merge",
    )(a, b, proj, proj, wa, wb)


def _outproj_router_kernel(x_ref, mg_ref, wout_ref, g_ref, wrh_ref, wrl_ref, br_ref,
                           h_ref, u_ref, ti_ref, tw_ref, *, n_experts):
    h1 = x_ref[...] + _dot(mg_ref[...], wout_ref[...])
    h_ref[...] = h1
    u = _rms(h1, g_ref[...])
    u_ref[...] = u
    uh, ul = _split2(u)
    logits = _dot(uh, wrh_ref[...]) + (_dot(ul, wrh_ref[...]) + _dot(uh, wrl_ref[...])) + br_ref[...]
    lane = lax.broadcasted_iota(I32, logits.shape, 1).astype(F32)
    vals = jnp.where(lane < n_experts, logits, -jnp.inf)
    top_v, top_i = [], []
    for _ in range(TOP_K):
        m = jnp.max(vals, axis=-1, keepdims=True)
        idx = jnp.min(jnp.where(vals == m, lane, float(LANES)), axis=-1, keepdims=True)
        top_v.append(m)
        top_i.append(idx)
        vals = jnp.where(lane == idx, -jnp.inf, vals)
    ex = [jnp.exp(v - top_v[0]) for v in top_v]
    den = ex[0]
    for e in ex[1:]:
        den = den + e
    tw = jnp.zeros(logits.shape, F32)
    ti = jnp.zeros(logits.shape, F32)
    for k in range(TOP_K):
        tw = jnp.where(lane == k, ex[k] / den, tw)
        ti = jnp.where(lane == k, top_i[k], ti)
    tw_ref[...] = tw
    ti_ref[...] = ti.astype(I32)


def outproj_router(x, merged, w_out, g_ffn, w_router, b_router):
    m, d = x.shape
    n_experts = w_router.shape[1]
    tm = ROW_TILE
    wr = jnp.zeros((d, LANES), F32).at[:, :n_experts].set(w_router.astype(F32))
    wr_hi = wr.astype(BF16)
    wr_lo = (wr - wr_hi.astype(F32)).astype(BF16)
    br = jnp.zeros((1, LANES), F32).at[0, :n_experts].set(b_router.astype(F32))
    row = lambda i: (i, 0)
    fixed = lambda i: (0, 0)
    kern = functools.partial(_outproj_router_kernel, n_experts=n_experts)
    return pl.pallas_call(
        kern,
        grid=(m // tm,),
        in_specs=[
            pl.BlockSpec((tm, d), row), pl.BlockSpec((tm, d), row), pl.BlockSpec((d, d), fixed),
            pl.BlockSpec((1, d), fixed), pl.BlockSpec((d, LANES), fixed), pl.BlockSpec((d, LANES), fixed),
            pl.BlockSpec((1, LANES), fixed),
        ],
        out_specs=[pl.BlockSpec((tm, d), row), pl.BlockSpec((tm, d), row),
                   pl.BlockSpec((tm, LANES), row), pl.BlockSpec((tm, LANES), row)],
        out_shape=[jax.ShapeDtypeStruct((m, d), F32), jax.ShapeDtypeStruct((m, d), F32),
                   jax.ShapeDtypeStruct((m, LANES), I32), jax.ShapeDtypeStruct((m, LANES), F32)],
        compiler_params=_cparams(("arbitrary",)),
        name="outproj_router",
    )(x, merged, w_out, g_ffn.reshape(1, d), wr_hi, wr_lo, br)


def _row_copy(src_hbm, dst, src_row, dst_row, sem):
    return pltpu.make_async_copy(src_hbm.at[pl.ds(src_row, 1)], dst.at[pl.ds(dst_row, 1)], sem)


def _moe_kernel(te_ref, tr_ref, idx_ref, u_hbm, wg_ref, wl_ref, bg_ref, bl_ref, wd_ref, bd_ref,
                y_ref, xg_scr, xb_scr, wg_scr, wl_scr, wd_scr, sem, *, sub_rows):
    s = pl.program_id(0)
    f = pl.program_id(1)
    n_rows = tr_ref[s]

    @pl.when((s == 0) & (f == 0))
    def _():
        xg_scr[...] = jnp.zeros_like(xg_scr)

    @pl.when(n_rows > 0)
    def _():
        @pl.when(f == 0)
        def _():
            n_groups = (n_rows + DMA_UNROLL - 1) // DMA_UNROLL

            def issue(gi, carry):
                for k in range(DMA_UNROLL):
                    r = gi * DMA_UNROLL + k
                    _row_copy(u_hbm, xg_scr, idx_ref[0, 0, r], r, sem).start()
                return carry

            lax.fori_loop(0, n_groups, issue, 0)

            def drain(gi, carry):
                for k in range(DMA_UNROLL):
                    _row_copy(u_hbm, xg_scr, 0, 0, sem).wait()
                return carry

            lax.fori_loop(0, n_groups, drain, 0)
            xb_scr[...] = xg_scr[...].astype(BF16)
            y_ref[...] = jnp.broadcast_to(bd_ref[0], y_ref.shape)

        wg_scr[...] = wg_ref[0].astype(BF16)
        wl_scr[...] = wl_ref[0].astype(BF16)
        wd_scr[...] = wd_ref[0].astype(BF16)
        bg = bg_ref[0]
        bl = bl_ref[0]

        def block(i, carry):
            r0 = pl.multiple_of(i * sub_rows, sub_rows)
            xb = xb_scr[pl.ds(r0, sub_rows), :]
            glu = jnp.minimum(_dot(xb, wg_scr[...]) + bg, SWIGLU_LIMIT)
            lin = jnp.clip(_dot(xb, wl_scr[...]) + bl, -SWIGLU_LIMIT, SWIGLU_LIMIT)
            act = glu * _sigmoid(SWIGLU_ALPHA * glu) * (lin + 1.0)
            y_ref[pl.ds(r0, sub_rows), :] += _dot(act.astype(BF16), wd_scr[...])
            return carry

        lax.fori_loop(0, (n_rows + sub_rows - 1) // sub_rows, block, 0)

    @pl.when((n_rows == 0) & (f == 0))
    def _():
        y_ref[...] = jnp.zeros_like(y_ref)


def routed_experts(u, tile_expert, tile_rows, src_tok, w_up, b_up, w_down, b_down):
    n_tiles, _, rows = src_tok.shape
    n_exp, d, two_ff = w_up.shape
    d_ff = two_ff // 2
    tf = MOE_FF_TILE
    nf = d_ff // tf

    def ff(s, f, tr):
        return jnp.where(tr[s] > 0, f, nf - 1)

    kern = functools.partial(_moe_kernel, sub_rows=MOE_SUB_ROWS)
    return pl.pallas_call(
        kern,
        grid_spec=pltpu.PrefetchScalarGridSpec(
            num_scalar_prefetch=2,
            grid=(n_tiles, nf),
            in_specs=[
                pl.BlockSpec((1, 1, rows), lambda s, f, te, tr: (s, 0, 0), memory_space=pltpu.SMEM),
                pl.BlockSpec(memory_space=pl.ANY),
                pl.BlockSpec((1, d, tf), lambda s, f, te, tr: (te[s], 0, ff(s, f, tr))),
                pl.BlockSpec((1, d, tf), lambda s, f, te, tr: (te[s], 0, nf + ff(s, f, tr))),
                pl.BlockSpec((1, 1, tf), lambda s, f, te, tr: (te[s], 0, ff(s, f, tr))),
                pl.BlockSpec((1, 1, tf), lambda s, f, te, tr: (te[s], 0, nf + ff(s, f, tr))),
                pl.BlockSpec((1, tf, d), lambda s, f, te, tr: (te[s], ff(s, f, tr), 0)),
                pl.BlockSpec((1, 1, d), lambda s, f, te, tr: (te[s], 0, 0)),
            ],
            out_specs=pl.BlockSpec((rows, d), lambda s, f, te, tr: (s, 0)),
            scratch_shapes=[
                pltpu.VMEM((rows, d), F32), pltpu.VMEM((rows, d), BF16),
                pltpu.VMEM((d, tf), BF16), pltpu.VMEM((d, tf), BF16), pltpu.VMEM((tf, d), BF16),
                pltpu.SemaphoreType.DMA(()),
            ],
        ),
        out_shape=jax.ShapeDtypeStruct((n_tiles * rows, d), F32),
        compiler_params=_cparams(("arbitrary", "arbitrary")),
        name="routed_experts",
    )(tile_expert, tile_rows, src_tok, u, w_up, w_up,
      b_up.reshape(n_exp, 1, two_ff), b_up.reshape(n_exp, 1, two_ff), w_down, b_down.reshape(n_exp, 1, d))


def routing_tables(top_idx, n_tokens, n_experts, rows):
    n_assign = n_tokens * TOP_K
    n_tiles = n_assign // rows + n_experts
    e_flat = top_idx[:n_tokens, :TOP_K].reshape(-1)
    onehot = (e_flat[:, None] == jnp.arange(n_experts, dtype=I32)[None, :]).astype(I32)
    csum = jnp.cumsum(onehot, axis=0)
    count = csum[-1]
    pos = jnp.sum(csum * onehot, axis=1) - 1
    tiles_of = (count + rows - 1) // rows
    tile_end = jnp.cumsum(tiles_of)
    tile_start = tile_end - tiles_of
    used = tile_end[-1]
    dest = tile_start[e_flat] * rows + pos
    tile_id = jnp.arange(n_tiles, dtype=I32)
    last = used - 1
    tile_block = jnp.minimum(tile_id, last)
    tile_expert = jnp.sum((tile_end[None, :] <= tile_block[:, None]).astype(I32), axis=1)
    tile_rows = jnp.clip(count[tile_expert] - (tile_id - tile_start[tile_expert]) * rows, 0, rows)
    tile_rows = jnp.where(tile_id < used, tile_rows, 0)
    token_of = jnp.arange(n_assign, dtype=I32) // TOP_K
    src_tok = jnp.zeros((n_tiles * rows,), I32).at[dest].set(token_of).reshape(n_tiles, 1, rows)
    return tile_expert.astype(I32), tile_rows.astype(I32), src_tok, dest.astype(I32)


def _final_kernel(dst_ref, dnext_ref, y_hbm, h_ref, tw_ref, p_ref, gple_ref, wgate_ref, wproj_ref, gfin_ref, o_ref,
                  buf, sems, *, tm, n_tiles):
    i = pl.program_id(0)
    slot = lax.rem(i, 2)
    rows_per_trip = DMA_UNROLL // TOP_K

    def fetch(table_ref, to_slot):
        def issue(gi, carry):
            for u in range(rows_per_trip):
                r = gi * rows_per_trip + u
                for k in range(TOP_K):
                    _row_copy(y_hbm, buf.at[to_slot, k], table_ref[0, 0, k * tm + r], r, sems.at[to_slot]).start()
            return carry

        lax.fori_loop(0, tm // rows_per_trip, issue, 0)

    @pl.when(i == 0)
    def _():
        fetch(dst_ref, 0)

    @pl.when(i + 1 < n_tiles)
    def _():
        fetch(dnext_ref, 1 - slot)

    def drain(gi, carry):
        for _ in range(DMA_UNROLL):
            _row_copy(y_hbm, buf.at[slot, 0], 0, 0, sems.at[slot]).wait()
        return carry

    lax.fori_loop(0, TOP_K * tm // DMA_UNROLL, drain, 0)
    tw = tw_ref[...]
    h2 = h_ref[...]
    for k in range(TOP_K):
        h2 = h2 + tw[:, k:k + 1] * buf[slot, k]
    un = _rms(h2, gple_ref[...]).astype(BF16)
    gate = _sigmoid(_dot(un, wgate_ref[...]))
    h3 = h2 + gate * _dot(p_ref[...].astype(BF16), wproj_ref[...])
    o_ref[...] = _rms(h3, gfin_ref[...])


def combine_ple_final(y_rows, dest, h1, top_w, p, g_ple, w_gate, w_proj, g_final):
    m, d = h1.shape
    tm = ROW_TILE
    pd = p.shape[1]
    row = lambda i: (i, 0)
    fixed = lambda i: (0, 0)
    n_tiles = m // tm
    kern = functools.partial(_final_kernel, tm=tm, n_tiles=n_tiles)
    return pl.pallas_call(
        kern,
        grid=(n_tiles,),
        in_specs=[
            pl.BlockSpec((1, 1, TOP_K * tm), lambda i: (i, 0, 0), memory_space=pltpu.SMEM),
            pl.BlockSpec((1, 1, TOP_K * tm), lambda i: (jnp.minimum(i + 1, n_tiles - 1), 0, 0),
                         memory_space=pltpu.SMEM),
            pl.BlockSpec(memory_space=pl.ANY),
            pl.BlockSpec((tm, d), row), pl.BlockSpec((tm, LANES), row), pl.BlockSpec((tm, pd), row),
            pl.BlockSpec((1, d), fixed), pl.BlockSpec((d, d), fixed), pl.BlockSpec((pd, d), fixed),
            pl.BlockSpec((1, d), fixed),
        ],
        out_specs=pl.BlockSpec((tm, d), row),
        out_shape=jax.ShapeDtypeStruct((m, d), F32),
        scratch_shapes=[pltpu.VMEM((2, TOP_K, tm, d), F32), pltpu.SemaphoreType.DMA((2,))],
        compiler_params=_cparams(("arbitrary",)),
        name="combine_ple_final",
    )(dest, dest, y_rows, h1, top_w, p, g_ple.reshape(1, d), w_gate, w_proj, g_final.reshape(1, d))


def kernel(x_prompt, x_sample, cache_k, cache_v, page_table, state_conv, state_ssm, p_prompt, p_sample, g_mix, w_in, b_sb, w_conv, a_log, dt_bias, g_onorm, w_o_sb, w_o_gdn, w_out, g_ffn, w_router, b_router, w_up, b_up, w_down, b_down, g_ple, w_ple_gate, w_ple_proj, g_final):
    bp, sp, d = x_prompt.shape
    bs = x_sample.shape[0]
    assert x_sample.shape[1] == 1 and g_mix.shape[0] == 1
    n_prompt = bp * sp
    n_tok = n_prompt + bs
    tp = -(-n_tok // MM_ROW_TILE) * MM_ROW_TILE
    width = N_HEADS * HEAD_DIM
    conv_w = 3 * width
    n_experts = w_router.shape[2]

    ba_lo = 3 * width + conv_w + width
    ba_hi = ba_lo + 2 * N_HEADS
    w_main = jnp.concatenate([w_in[0][:, :ba_lo], w_in[0][:, ba_hi:]], axis=1).astype(BF16)
    w_ba = jnp.zeros((d, LANES), BF16).at[:, :2 * N_HEADS].set(w_in[0][:, ba_lo:ba_hi].astype(BF16))

    x_all = jnp.concatenate([x_prompt.reshape(n_prompt, d), x_sample.reshape(bs, d),
                             jnp.zeros((tp - n_tok, d), x_prompt.dtype)], axis=0)
    xn = rmsnorm_cast(x_all, g_mix[0], ROW_TILE)
    proj = matmul(xn, w_main, MM_ROW_TILE, MM_COL_TILE, "in_proj")
    ba = matmul(xn, w_ba, MM_ROW_TILE, LANES, "in_proj_gates")

    heads = lambda t, b, s: t.reshape(1, b, s, N_HEADS, HEAD_DIM)
    k_prompt = heads(proj[:n_prompt, width:2 * width], bp, sp)
    v_prompt = heads(proj[:n_prompt, 2 * width:3 * width], bp, sp)
    k_sample = heads(proj[n_prompt:n_tok, width:2 * width], bs, 1)
    v_sample = heads(proj[n_prompt:n_tok, 2 * width:3 * width], bs, 1)

    o_sb_p = sb_attention_prompt(proj[:n_prompt, :3 * width].astype(BF16), b_sb[0].astype(F32), bp, sp)
    n_pool, page = cache_k.shape[1], cache_k.shape[2]
    o_sb_s = sb_attention_decode(proj[n_prompt:n_tok, :width], b_sb[0].astype(F32),
                                 cache_k.reshape(n_pool, page * N_HEADS, HEAD_DIM),
                                 cache_v.reshape(n_pool, page * N_HEADS, HEAD_DIM), page_table)
    o_sb = jnp.concatenate([o_sb_p, o_sb_s.astype(BF16), jnp.zeros((tp - n_tok, width), BF16)], axis=0)

    conv_blk = 3 * width // conv_w
    z_blk = (3 * width + conv_w) // width
    og_p, ssm_p = gated_deltanet(
        proj, conv_blk, proj, z_blk, ba, jnp.zeros((bp, CONV_TAPS - 1, conv_w), F32),
        jnp.zeros((bp, N_HEADS, HEAD_DIM, HEAD_DIM), F32), w_conv[0], a_log[0], dt_bias[0], g_onorm[0],
        batch=bp, seq=sp, chunk=GDN_CHUNK, valid_rows=GDN_CHUNK)
    pad_rows = GDN_DECODE_ROWS
    spread = lambda t: jnp.zeros((bs, pad_rows, t.shape[1]), F32).at[:, 0].set(t).reshape(bs * pad_rows, t.shape[1])
    conv_in_s = proj[n_prompt:n_tok, 3 * width:3 * width + conv_w]
    og_s, ssm_s = gated_deltanet(
        spread(conv_in_s), 0, spread(proj[n_prompt:n_tok, 3 * width + conv_w:ba_lo]), 0, spread(ba[n_prompt:n_tok]),
        state_conv[0].astype(F32), state_ssm[0].astype(F32), w_conv[0], a_log[0], dt_bias[0], g_onorm[0],
        batch=bs, seq=pad_rows, chunk=pad_rows, valid_rows=1)
    o_g = jnp.concatenate([og_p, og_s.reshape(bs, pad_rows, width)[:, 0], jnp.zeros((tp - n_tok, width), BF16)], axis=0)
    conv_prompt = proj[:n_prompt, 3 * width:3 * width + conv_w].reshape(bp, sp, conv_w)[:, sp - (CONV_TAPS - 1):][None]
    conv_sample = jnp.concatenate([state_conv[0][:, 1:].astype(F32), conv_in_s[:, None]], axis=1)[None]

    gate_blk = ba_lo // MM_COL_TILE
    merged = gated_merge(o_sb, o_g, proj, gate_blk, gate_blk + d // MM_COL_TILE,
                         w_o_sb[0].astype(BF16), w_o_gdn[0].astype(BF16))
    h1, u2, top_i, top_w = outproj_router(x_all, merged, w_out[0].astype(BF16), g_ffn[0], w_router[0], b_router[0])

    te, tr, src_tok, dest = routing_tables(top_i, n_tok, n_experts, MOE_TILE_ROWS)
    y_rows = routed_experts(u2, te, tr, src_tok, w_up[0], b_up[0], w_down[0], b_down[0])

    dest_tok = jnp.zeros((tp, TOP_K), I32).at[:n_tok].set(dest.reshape(n_tok, TOP_K))
    dest_tiles = dest_tok.reshape(tp // ROW_TILE, ROW_TILE, TOP_K).transpose(0, 2, 1).reshape(tp // ROW_TILE, 1, TOP_K * ROW_TILE)
    p_all = jnp.concatenate([p_prompt[0].reshape(n_prompt, -1), p_sample[0].reshape(bs, -1),
                             jnp.zeros((tp - n_tok, p_prompt.shape[-1]), p_prompt.dtype)], axis=0)
    y = combine_ple_final(y_rows, dest_tiles, h1, top_w, p_all, g_ple[0], w_ple_gate[0].astype(BF16),
                          w_ple_proj[0].astype(BF16), g_final)

    y_prompt = y[:n_prompt].reshape(bp, sp, d)
    y_sample = y[n_prompt:n_tok].reshape(bs, 1, d)
    return (y_prompt, y_sample, k_prompt, v_prompt, k_sample, v_sample,
            conv_prompt, ssm_p[None], conv_sample, ssm_s[None])
```

```python
import functools

import jax
import jax.numpy as jnp
from jax import lax
from jax.experimental import pallas as pl
from jax.experimental.pallas import tpu as pltpu

F32 = jnp.float32
BF16 = jnp.bfloat16
I32 = jnp.int32

EPS = 1e-6
LANES = 128
HEAD_DIM = 128
N_HEADS = 8
CONV_TAPS = 4
GDN_CHUNK = 64
GDN_DECODE_ROWS = 16
TOP_K = 4
SWIGLU_ALPHA = 1.702
SWIGLU_LIMIT = 7.0
VMEM_LIMIT = 56 * 1024 * 1024

ROW_TILE = 256
MM_ROW_TILE = 512
MM_COL_TILE = 1024
ATT_TILE = 256
ATT_HEAD_GROUP = 4
DECODE_PAGE_GROUP = 8
MOE_TILE_ROWS = 1280
MOE_SUB_ROWS = 256
MOE_FF_TILE = 256
DMA_UNROLL = 8


def _cparams(sem):
    return pltpu.CompilerParams(dimension_semantics=sem, vmem_limit_bytes=VMEM_LIMIT)


def _sigmoid(x):
    return 1.0 / (1.0 + jnp.exp(-x))


def _softplus_neg_abs(x):
    return jnp.log(1.0 + jnp.exp(-jnp.abs(x)))


def _split2(x):
    hi = x.astype(BF16)
    lo = (x - hi.astype(F32)).astype(BF16)
    return hi, lo


def _split3(x):
    hi = x.astype(BF16)
    r = x - hi.astype(F32)
    mid = r.astype(BF16)
    lo = (r - mid.astype(F32)).astype(BF16)
    return hi, mid, lo


def _dot(a, b):
    return jnp.dot(a, b, preferred_element_type=F32)


def _dot_nt(a, b):
    return lax.dot_general(a, b, (((1,), (1,)), ((), ())), preferred_element_type=F32)


def _dot_tn(a, b):
    return lax.dot_general(a, b, (((0,), (0,)), ((), ())), preferred_element_type=F32)


def _dot_wide(a, b):
    ah, al = _split2(a)
    bh, bl = _split2(b)
    return _dot(ah, bh) + (_dot(ah, bl) + _dot(al, bh))


def _dot_exact_lhs(a_bf16, x):
    hi, mid, lo = _split3(x)
    return _dot(a_bf16, hi) + (_dot(a_bf16, mid) + _dot(a_bf16, lo))


def _rms(x, g):
    return x * lax.rsqrt(jnp.mean(x * x, axis=-1, keepdims=True) + EPS) * g


def _rms_kernel(x_ref, g_ref, o_ref):
    o_ref[...] = _rms(x_ref[...], g_ref[...]).astype(o_ref.dtype)


def rmsnorm_cast(x, g, tm):
    m, d = x.shape
    return pl.pallas_call(
        _rms_kernel,
        grid=(m // tm,),
        in_specs=[pl.BlockSpec((tm, d), lambda i: (i, 0)), pl.BlockSpec((1, d), lambda i: (0, 0))],
        out_specs=pl.BlockSpec((tm, d), lambda i: (i, 0)),
        out_shape=jax.ShapeDtypeStruct((m, d), BF16),
        compiler_params=_cparams(("arbitrary",)),
        name="rmsnorm_cast",
    )(x, g.reshape(1, d))


def _mm_kernel(a_ref, w_ref, *o_refs):
    acc = _dot(a_ref[...], w_ref[...])
    for o_ref in o_refs:
        o_ref[...] = acc.astype(o_ref.dtype)


def matmul(a, w, tm, tn, name, out_dtypes=(F32,), row_start=0, rows=None):
    k = a.shape[1]
    n = w.shape[1]
    rows = a.shape[0] if rows is None else rows
    first = row_start // tm
    outs = pl.pallas_call(
        _mm_kernel,
        grid=(n // tn, rows // tm),
        in_specs=[pl.BlockSpec((tm, k), lambda j, i: (first + i, 0)), pl.BlockSpec((k, tn), lambda j, i: (0, j))],
        out_specs=[pl.BlockSpec((tm, tn), lambda j, i: (i, j)) for _ in out_dtypes],
        out_shape=[jax.ShapeDtypeStruct((rows, n), dt) for dt in out_dtypes],
        compiler_params=_cparams(("arbitrary", "arbitrary")),
        name=name,
    )(a, w)
    return outs[0] if len(outs) == 1 else outs


def _suffix_matrix():
    j = lax.broadcasted_iota(I32, (LANES, 2 * LANES), 0)
    s = lax.broadcasted_iota(I32, (LANES, 2 * LANES), 1)
    return jnp.where((s >= LANES) | (j > s), 1.0, 0.0).astype(BF16)


def _sb_prompt_kernel(bias_ref, q_ref, k_ref, v_ref, cum_ref, o_ref, *, tile, scale, group):
    hg = pl.program_id(1)
    i = pl.program_id(2)
    heads = range(group)
    col = lambda h: slice(h * HEAD_DIM, (h + 1) * HEAD_DIM)
    q = [q_ref[:, col(h)] for h in heads]
    bias = [bias_ref[hg * group + h] for h in heads]
    cum = cum_ref[...]
    rows = lax.broadcasted_iota(I32, (tile, LANES), 0)
    cols = lax.broadcasted_iota(I32, (tile, LANES), 1)
    nsub = tile // LANES

    def block(j, carry, diagonal):
        run, acc = carry
        k0 = pl.multiple_of(j * tile, tile)
        s = [_dot_nt(q[h], k_ref[pl.ds(k0, tile), col(h)]) * scale + bias[h] for h in heads]
        for sb in reversed(range(nsub)):
            sub = [s[h][:, sb * LANES:(sb + 1) * LANES] for h in heads]
            t = [_softplus_neg_abs(x) for x in sub]
            log_beta = [jnp.minimum(x, 0.0) - tt for x, tt in zip(sub, t)]
            log_keep = [lb - x for lb, x in zip(log_beta, sub)]
            if diagonal:
                vis = (cols + sb * LANES) < rows
                log_keep = [jnp.where(vis, x, 0.0) for x in log_keep]
            c = [_dot(x.astype(BF16), cum) for x in log_keep]
            w = [jnp.exp(log_beta[h] + (c[h][:, :LANES] + run[h])) for h in heads]
            if diagonal:
                w = [jnp.where(vis, x, 0.0) for x in w]
            run = [run[h] + c[h][:, LANES:] for h in heads]
            ks = pl.multiple_of(k0 + sb * LANES, LANES)
            acc = [acc[h] + _dot(w[h].astype(BF16), v_ref[pl.ds(ks, LANES), col(h)]) for h in heads]
        return run, acc

    zero = [jnp.zeros((tile, LANES), F32) for _ in heads]
    carry = block(i, (zero, zero), True)
    _, acc = lax.fori_loop(1, i + 1, lambda jj, c: block(i - jj, c, False), carry)
    for h in heads:
        o_ref[:, col(h)] = acc[h].astype(o_ref.dtype)


def sb_attention_prompt(q, k, v, bias, batch, seq):
    tile = ATT_TILE
    group = ATT_HEAD_GROUP
    nq = seq // tile
    gw = group * HEAD_DIM
    kern = functools.partial(_sb_prompt_kernel, tile=tile, scale=HEAD_DIM ** -0.5, group=group)
    return pl.pallas_call(
        kern,
        grid=(batch, N_HEADS // group, nq),
        in_specs=[
            pl.BlockSpec(memory_space=pltpu.SMEM),
            pl.BlockSpec((tile, gw), lambda b, g, i: (b * nq + i, g)),
            pl.BlockSpec((seq, gw), lambda b, g, i: (b, g)),
            pl.BlockSpec((seq, gw), lambda b, g, i: (b, g)),
            pl.BlockSpec((LANES, 2 * LANES), lambda b, g, i: (0, 0)),
        ],
        out_specs=pl.BlockSpec((tile, gw), lambda b, g, i: (b * nq + i, g)),
        out_shape=jax.ShapeDtypeStruct((batch * seq, N_HEADS * HEAD_DIM), BF16),
        compiler_params=_cparams(("arbitrary", "arbitrary", "arbitrary")),
        name="sb_attention_prompt",
    )(bias, q, k, v, _suffix_matrix())


def _sb_decode_kernel(pt_ref, qt_ref, bias_ref, *refs, scale, n_steps, group, page):
    k_refs, v_refs = refs[:group], refs[group:2 * group]
    cumt_ref, expand_ref, o_ref, z_scr, run_scr, acc_scr = refs[2 * group:]
    jj = pl.program_id(1)

    @pl.when(jj == 0)
    def _():
        run_scr[...] = jnp.zeros_like(run_scr)
        acc_scr[...] = jnp.zeros_like(acc_scr)

    pages = range(group)
    lane = lax.broadcasted_iota(I32, (page, LANES), 1)
    qt = qt_ref[0]
    bias = bias_ref[...]
    for g in pages:
        z_scr[g] = _dot(k_refs[g][0].astype(BF16), qt)
    s = []
    for g in pages:
        acc = jnp.zeros((page, LANES), F32)
        for h in range(N_HEADS):
            acc = jnp.where(lane == h, z_scr[g, pl.ds(h, page, stride=N_HEADS), :], acc)
        s.append(acc * scale + bias)
    t = [_softplus_neg_abs(x) for x in s]
    log_beta = [jnp.minimum(x, 0.0) - tt for x, tt in zip(s, t)]
    log_keep = [lb - x for lb, x in zip(log_beta, s)]
    c = [_dot(cumt_ref[...], x.astype(BF16)) for x in log_keep]
    run = run_scr[...]
    w = []
    for g in pages:
        w.append(jnp.exp(log_beta[g] + (c[g][:page] + run)))
        run = run + c[g][page:]
    run_scr[...] = run
    wide = [_dot(x.astype(BF16), expand_ref[...]) for x in w]
    for h in range(N_HEADS):
        tot = acc_scr[h]
        for g in pages:
            vh = v_refs[g][0, pl.ds(h, page, stride=N_HEADS), :]
            prod = wide[g][:, h * HEAD_DIM:(h + 1) * HEAD_DIM] * vh
            tot = tot + jnp.sum(prod.reshape(page // 8, 8, HEAD_DIM), axis=0)
        acc_scr[h] = tot

    @pl.when(jj == n_steps - 1)
    def _():
        o_ref[0] = jnp.concatenate([jnp.sum(acc_scr[h], axis=0, keepdims=True) for h in range(N_HEADS)], axis=0)


def sb_attention_decode(q, bias, cache_k, cache_v, page_table):
    b, width = q.shape
    n_pages = page_table.shape[1]
    page = cache_k.shape[1] // N_HEADS
    group = DECODE_PAGE_GROUP
    assert page == LANES and n_pages % group == 0 and cache_k.shape[2] == HEAD_DIM
    n_steps = n_pages // group
    qt = jnp.zeros((b, HEAD_DIM, LANES), F32).at[:, :, :N_HEADS].set(
        q.reshape(b, N_HEADS, HEAD_DIM).transpose(0, 2, 1)).astype(BF16)
    bias_row = jnp.zeros((1, LANES), F32).at[0, :N_HEADS].set(bias)
    r = lax.broadcasted_iota(I32, (2 * page, page), 0)
    cidx = lax.broadcasted_iota(I32, (2 * page, page), 1)
    cumt = jnp.where((r >= page) | (cidx > r), 1.0, 0.0).astype(BF16)
    lane = lax.broadcasted_iota(I32, (LANES, width), 0)
    col_head = lax.broadcasted_iota(I32, (LANES, width), 1) // HEAD_DIM
    expand = jnp.where(lane == col_head, 1.0, 0.0).astype(BF16)

    def page_spec(g):
        return pl.BlockSpec((1, page * N_HEADS, HEAD_DIM),
                            lambda i, j, pt: (pt[i, n_pages - 1 - (j * group + g)], 0, 0))

    kern = functools.partial(_sb_decode_kernel, scale=HEAD_DIM ** -0.5, n_steps=n_steps, group=group, page=page)
    out = pl.pallas_call(
        kern,
        grid_spec=pltpu.PrefetchScalarGridSpec(
            num_scalar_prefetch=1,
            grid=(b, n_steps),
            in_specs=[pl.BlockSpec((1, HEAD_DIM, LANES), lambda i, j, pt: (i, 0, 0)),
                      pl.BlockSpec((1, LANES), lambda i, j, pt: (0, 0))]
                     + [page_spec(g) for g in range(group)] * 2
                     + [pl.BlockSpec((2 * page, page), lambda i, j, pt: (0, 0)),
                        pl.BlockSpec((LANES, width), lambda i, j, pt: (0, 0))],
            out_specs=pl.BlockSpec((1, N_HEADS, HEAD_DIM), lambda i, j, pt: (i, 0, 0)),
            scratch_shapes=[pltpu.VMEM((group, page * N_HEADS, LANES), F32), pltpu.VMEM((page, LANES), F32),
                            pltpu.VMEM((N_HEADS, 8, HEAD_DIM), F32)],
        ),
        out_shape=jax.ShapeDtypeStruct((b, N_HEADS, HEAD_DIM), F32),
        compiler_params=_cparams(("arbitrary", "arbitrary")),
        name="sb_attention_decode",
    )(page_table, qt, bias_row, *([cache_k] * group), *([cache_v] * group), cumt, expand)
    return out.reshape(b, width)


def _inv_unit_lower(mats, size):
    rr = lax.broadcasted_iota(I32, mats[0].shape, 0)
    cc = lax.broadcasted_iota(I32, mats[0].shape, 1)
    eye = jnp.where(rr == cc, 1.0, 0.0)
    t = [eye - n for n in mats]
    p = list(mats)
    span = 2
    while span < size:
        p = [_dot_wide(x, x) for x in p]
        t = [x + _dot_wide(x, y) for x, y in zip(t, p)]
        span *= 2
    return t


def _gdn_kernel(xc_ref, z_ref, ba_ref, hist_ref, s0_ref, wconv_ref, arow_ref, dtrow_ref, gon_ref, lmat_ref,
                og_ref, sout_ref, xp_scr, s_scr, *, chunk, n_chunks, valid_rows):
    c = pl.program_id(1)
    qk_w = N_HEADS * HEAD_DIM

    @pl.when(c == 0)
    def _():
        xp_scr[5:8, :] = hist_ref[0]
        s_scr[...] = s0_ref[0]

    x = xc_ref[...]
    xp_scr[8:8 + chunk, :] = x
    wc = wconv_ref[...]
    conv = xp_scr[5:5 + chunk, :] * wc[0:1, :]
    conv = conv + xp_scr[6:6 + chunk, :] * wc[1:2, :]
    conv = conv + xp_scr[7:7 + chunk, :] * wc[2:3, :]
    conv = conv + x * wc[3:4, :]
    xp_scr[5:8, :] = x[chunk - 3:chunk, :]
    conv = conv * _sigmoid(conv)

    ba = ba_ref[...]
    beta_all = _sigmoid(ba)
    sp = ba + dtrow_ref[...]
    g_all = -arow_ref[...] * (jnp.maximum(sp, 0.0) + _softplus_neg_abs(sp))
    if valid_rows < chunk:
        live = lax.broadcasted_iota(I32, ba.shape, 0) < valid_rows
        beta_all = jnp.where(live, beta_all, 0.0)
        g_all = jnp.where(live, g_all, 0.0)
    stacked = _dot_exact_lhs(lmat_ref[...], g_all)
    gc_all = stacked[:chunk]
    gc_t = stacked[:LANES].T
    g_tot = stacked[LANES:]

    ii = lax.broadcasted_iota(I32, (chunk, chunk), 0)
    jj = lax.broadcasted_iota(I32, (chunk, chunk), 1)
    incl = ii >= jj
    strict = ii > jj
    gon = gon_ref[...]

    heads = range(N_HEADS)
    col = lambda h, part: slice(part * qk_w + h * HEAD_DIM, part * qk_w + (h + 1) * HEAD_DIM)
    l2n = lambda x: x * lax.rsqrt(jnp.sum(x * x, axis=-1, keepdims=True) + EPS)
    q = [l2n(conv[:, col(h, 0)]) * (HEAD_DIM ** -0.5) for h in heads]
    k = [l2n(conv[:, col(h, 1)]) for h in heads]
    v = [conv[:, col(h, 2)] for h in heads]
    gate_lane = lambda h: slice(N_HEADS + h, N_HEADS + h + 1)
    bcol = [beta_all[:, h:h + 1] for h in heads]
    gcol = [gc_all[:, gate_lane(h)] for h in heads]
    grow = [gc_t[gate_lane(h), :chunk] for h in heads]
    glast_c = [g_tot[:chunk, gate_lane(h)] for h in heads]
    glast_s = [g_tot[:, gate_lane(h)] for h in heads]
    decay = [jnp.where(incl, jnp.exp(jnp.where(incl, gcol[h] - grow[h], 0.0)), 0.0) for h in heads]
    kb = [k[h] * bcol[h] for h in heads]
    k_b16 = [x.astype(BF16) for x in k]
    n_mat = [jnp.where(strict, _dot_nt(kb[h].astype(BF16), k_b16[h]) * decay[h], 0.0) for h in heads]
    t_inv = _inv_unit_lower(n_mat, chunk)
    rhs = [jnp.concatenate([v[h] * bcol[h], kb[h] * jnp.exp(gcol[h])], axis=1) for h in heads]
    sol = [_dot_wide(t_inv[h], rhs[h]) for h in heads]
    qk = [_dot_nt(q[h].astype(BF16), k_b16[h]) * decay[h] for h in heads]
    q_dec = [(q[h] * jnp.exp(gcol[h])).astype(BF16) for h in heads]
    k_end = [(k[h] * jnp.exp(glast_c[h] - gcol[h])).astype(BF16) for h in heads]
    state = [s_scr[h] for h in heads]
    state_b = [x.astype(BF16) for x in state]
    v_new = [sol[h][:, :HEAD_DIM] - _dot(sol[h][:, HEAD_DIM:].astype(BF16), state_b[h]) for h in heads]
    v_new_b = [x.astype(BF16) for x in v_new]
    o = [_dot(q_dec[h], state_b[h]) + _dot(qk[h].astype(BF16), v_new_b[h]) for h in heads]
    new_state = [state[h] * jnp.exp(glast_s[h]) + _dot_tn(k_end[h], v_new_b[h]) for h in heads]
    for h in heads:
        s_scr[h] = new_state[h]
    for h in heads:
        zh = z_ref[:, col(h, 0)]
        og_ref[:, col(h, 0)] = (_rms(o[h], gon) * (zh * _sigmoid(zh))).astype(og_ref.dtype)

    @pl.when(c == n_chunks - 1)
    def _():
        sout_ref[0] = s_scr[...]


def gated_deltanet(xc, xc_col, zz, z_col, ba, hist, s0, w_conv, a_log, dt_bias, g_onorm, *,
                   batch, seq, chunk, valid_rows):
    n_chunks = seq // chunk
    width = N_HEADS * HEAD_DIM
    a_row = jnp.zeros((1, LANES), F32).at[0, N_HEADS:2 * N_HEADS].set(jnp.exp(a_log.astype(F32)))
    dt_row = jnp.zeros((1, LANES), F32).at[0, N_HEADS:2 * N_HEADS].set(dt_bias.astype(F32))
    r = lax.broadcasted_iota(I32, (2 * LANES, chunk), 0)
    t = lax.broadcasted_iota(I32, (2 * LANES, chunk), 1)
    lmat = jnp.where((r >= LANES) | (t <= r), 1.0, 0.0).astype(BF16)
    kern = functools.partial(_gdn_kernel, chunk=chunk, n_chunks=n_chunks, valid_rows=valid_rows)
    return pl.pallas_call(
        kern,
        grid=(batch, n_chunks),
        in_specs=[
            pl.BlockSpec((chunk, 3 * width), lambda b, c: (b * n_chunks + c, xc_col)),
            pl.BlockSpec((chunk, width), lambda b, c: (b * n_chunks + c, z_col)),
            pl.BlockSpec((chunk, LANES), lambda b, c: (b * n_chunks + c, 0)),
            pl.BlockSpec((1, CONV_TAPS - 1, 3 * width), lambda b, c: (b, 0, 0)),
            pl.BlockSpec((1, N_HEADS, HEAD_DIM, HEAD_DIM), lambda b, c: (b, 0, 0, 0)),
            pl.BlockSpec((CONV_TAPS, 3 * width), lambda b, c: (0, 0)),
            pl.BlockSpec((1, LANES), lambda b, c: (0, 0)),
            pl.BlockSpec((1, LANES), lambda b, c: (0, 0)),
            pl.BlockSpec((1, HEAD_DIM), lambda b, c: (0, 0)),
            pl.BlockSpec((2 * LANES, chunk), lambda b, c: (0, 0)),
        ],
        out_specs=[
            pl.BlockSpec((chunk, width), lambda b, c: (b * n_chunks + c, 0)),
            pl.BlockSpec((1, N_HEADS, HEAD_DIM, HEAD_DIM), lambda b, c: (b, 0, 0, 0)),
        ],
        out_shape=[
            jax.ShapeDtypeStruct((batch * seq, width), BF16),
            jax.ShapeDtypeStruct((batch, N_HEADS, HEAD_DIM, HEAD_DIM), F32),
        ],
        scratch_shapes=[pltpu.VMEM((8 + chunk, 3 * width), F32), pltpu.VMEM((N_HEADS, HEAD_DIM, HEAD_DIM), F32)],
        compiler_params=_cparams(("arbitrary", "arbitrary")),
        name="gated_deltanet",
    )(xc, zz, ba, hist, s0, w_conv, a_row, dt_row, g_onorm.reshape(1, HEAD_DIM), lmat)


def _merge_kernel(a_ref, b_ref, ga_ref, gb_ref, wa_ref, wb_ref, o_ref):
    ya = _dot(a_ref[...], wa_ref[...])
    yb = _dot(b_ref[...], wb_ref[...])
    o_ref[...] = (_sigmoid(ga_ref[...]) * ya + _sigmoid(gb_ref[...]) * yb).astype(o_ref.dtype)


def gated_merge(a, b, proj, ga_col, gb_col, wa, wb):
    m, k = a.shape
    n = wa.shape[1]
    tm, tn = MM_ROW_TILE, MM_COL_TILE
    return pl.pallas_call(
        _merge_kernel,
        grid=(n // tn, m // tm),
        in_specs=[
            pl.BlockSpec((tm, k), lambda j, i: (i, 0)),
            pl.BlockSpec((tm, k), lambda j, i: (i, 0)),
            pl.BlockSpec((tm, tn), lambda j, i: (i, ga_col + j)),
            pl.BlockSpec((tm, tn), lambda j, i: (i, gb_col + j)),
            pl.BlockSpec((k, tn), lambda j, i: (0, j)),
            pl.BlockSpec((k, tn), lambda j, i: (0, j)),
        ],
        out_specs=pl.BlockSpec((tm, tn), lambda j, i: (i, j)),
        out_shape=jax.ShapeDtypeStruct((m, n), BF16),
        compiler_params=_cparams(("arbitrary", "arbitrary")),
        name="gated_merge",
    )(a, b, proj, proj, wa, wb)


def _outproj_router_kernel(x_ref, mg_ref, wout_ref, g_ref, wrh_ref, wrl_ref, br_ref,
                           h_ref, u_ref, ti_ref, tw_ref, *, n_experts):
    h1 = x_ref[...] + _dot(mg_ref[...], wout_ref[...])
    h_ref[...] = h1
    u = _rms(h1, g_ref[...])
    u_ref[...] = u
    uh, ul = _split2(u)
    logits = _dot(uh, wrh_ref[...]) + (_dot(ul, wrh_ref[...]) + _dot(uh, wrl_ref[...])) + br_ref[...]
    lane = lax.broadcasted_iota(I32, logits.shape, 1).astype(F32)
    vals = jnp.where(lane < n_experts, logits, -jnp.inf)
    top_v, top_i = [], []
    for _ in range(TOP_K):
        m = jnp.max(vals, axis=-1, keepdims=True)
        idx = jnp.min(jnp.where(vals == m, lane, float(LANES)), axis=-1, keepdims=True)
        top_v.append(m)
        top_i.append(idx)
        vals = jnp.where(lane == idx, -jnp.inf, vals)
    ex = [jnp.exp(v - top_v[0]) for v in top_v]
    den = ex[0]
    for e in ex[1:]:
        den = den + e
    tw = jnp.zeros(logits.shape, F32)
    ti = jnp.zeros(logits.shape, F32)
    for k in range(TOP_K):
        tw = jnp.where(lane == k, ex[k] / den, tw)
        ti = jnp.where(lane == k, top_i[k], ti)
    tw_ref[...] = tw
    ti_ref[...] = ti.astype(I32)


def outproj_router(x, merged, w_out, g_ffn, w_router, b_router):
    m, d = x.shape
    n_experts = w_router.shape[1]
    tm = ROW_TILE
    wr = jnp.zeros((d, LANES), F32).at[:, :n_experts].set(w_router.astype(F32))
    wr_hi = wr.astype(BF16)
    wr_lo = (wr - wr_hi.astype(F32)).astype(BF16)
    br = jnp.zeros((1, LANES), F32).at[0, :n_experts].set(b_router.astype(F32))
    row = lambda i: (i, 0)
    fixed = lambda i: (0, 0)
    kern = functools.partial(_outproj_router_kernel, n_experts=n_experts)
    return pl.pallas_call(
        kern,
        grid=(m // tm,),
        in_specs=[
            pl.BlockSpec((tm, d), row), pl.BlockSpec((tm, d), row), pl.BlockSpec((d, d), fixed),
            pl.BlockSpec((1, d), fixed), pl.BlockSpec((d, LANES), fixed), pl.BlockSpec((d, LANES), fixed),
            pl.BlockSpec((1, LANES), fixed),
        ],
        out_specs=[pl.BlockSpec((tm, d), row), pl.BlockSpec((tm, d), row),
                   pl.BlockSpec((tm, LANES), row), pl.BlockSpec((tm, LANES), row)],
        out_shape=[jax.ShapeDtypeStruct((m, d), F32), jax.ShapeDtypeStruct((m, d), F32),
                   jax.ShapeDtypeStruct((m, LANES), I32), jax.ShapeDtypeStruct((m, LANES), F32)],
        compiler_params=_cparams(("arbitrary",)),
        name="outproj_router",
    )(x, merged, w_out, g_ffn.reshape(1, d), wr_hi, wr_lo, br)


def _row_copy(src_hbm, dst, src_row, dst_row, sem):
    return pltpu.make_async_copy(src_hbm.at[pl.ds(src_row, 1)], dst.at[pl.ds(dst_row, 1)], sem)


def _moe_kernel(te_ref, tr_ref, idx_ref, u_hbm, wg_ref, wl_ref, bg_ref, bl_ref, wd_ref, bd_ref,
                y_ref, xg_scr, xb_scr, wg_scr, wl_scr, wd_scr, sem, *, sub_rows):
    s = pl.program_id(0)
    f = pl.program_id(1)
    n_rows = tr_ref[s]

    @pl.when((s == 0) & (f == 0))
    def _():
        xg_scr[...] = jnp.zeros_like(xg_scr)

    @pl.when(n_rows > 0)
    def _():
        @pl.when(f == 0)
        def _():
            n_groups = (n_rows + DMA_UNROLL - 1) // DMA_UNROLL

            def issue(gi, carry):
                for k in range(DMA_UNROLL):
                    r = gi * DMA_UNROLL + k
                    _row_copy(u_hbm, xg_scr, idx_ref[0, 0, r], r, sem).start()
                return carry

            lax.fori_loop(0, n_groups, issue, 0)

            def drain(gi, carry):
                for k in range(DMA_UNROLL):
                    _row_copy(u_hbm, xg_scr, 0, 0, sem).wait()
                return carry

            lax.fori_loop(0, n_groups, drain, 0)
            xb_scr[...] = xg_scr[...].astype(BF16)
            y_ref[...] = jnp.broadcast_to(bd_ref[0], y_ref.shape)

        wg_scr[...] = wg_ref[0].astype(BF16)
        wl_scr[...] = wl_ref[0].astype(BF16)
        wd_scr[...] = wd_ref[0].astype(BF16)
        bg = bg_ref[0]
        bl = bl_ref[0]

        def block(i, carry):
            r0 = pl.multiple_of(i * sub_rows, sub_rows)
            xb = xb_scr[pl.ds(r0, sub_rows), :]
            glu = jnp.minimum(_dot(xb, wg_scr[...]) + bg, SWIGLU_LIMIT)
            lin = jnp.clip(_dot(xb, wl_scr[...]) + bl, -SWIGLU_LIMIT, SWIGLU_LIMIT)
            act = glu * _sigmoid(SWIGLU_ALPHA * glu) * (lin + 1.0)
            y_ref[pl.ds(r0, sub_rows), :] += _dot(act.astype(BF16), wd_scr[...])
            return carry

        lax.fori_loop(0, (n_rows + sub_rows - 1) // sub_rows, block, 0)

    @pl.when((n_rows == 0) & (f == 0))
    def _():
        y_ref[...] = jnp.zeros_like(y_ref)


def routed_experts(u, tile_expert, tile_rows, src_tok, w_up, b_up, w_down, b_down):
    n_tiles, _, rows = src_tok.shape
    n_exp, d, two_ff = w_up.shape
    d_ff = two_ff // 2
    tf = MOE_FF_TILE
    nf = d_ff // tf

    def ff(s, f, tr):
        return jnp.where(tr[s] > 0, f, nf - 1)

    kern = functools.partial(_moe_kernel, sub_rows=MOE_SUB_ROWS)
    return pl.pallas_call(
        kern,
        grid_spec=pltpu.PrefetchScalarGridSpec(
            num_scalar_prefetch=2,
            grid=(n_tiles, nf),
            in_specs=[
                pl.BlockSpec((1, 1, rows), lambda s, f, te, tr: (s, 0, 0), memory_space=pltpu.SMEM),
                pl.BlockSpec(memory_space=pl.ANY),
                pl.BlockSpec((1, d, tf), lambda s, f, te, tr: (te[s], 0, ff(s, f, tr))),
                pl.BlockSpec((1, d, tf), lambda s, f, te, tr: (te[s], 0, nf + ff(s, f, tr))),
                pl.BlockSpec((1, 1, tf), lambda s, f, te, tr: (te[s], 0, ff(s, f, tr))),
                pl.BlockSpec((1, 1, tf), lambda s, f, te, tr: (te[s], 0, nf + ff(s, f, tr))),
                pl.BlockSpec((1, tf, d), lambda s, f, te, tr: (te[s], ff(s, f, tr), 0)),
                pl.BlockSpec((1, 1, d), lambda s, f, te, tr: (te[s], 0, 0)),
            ],
            out_specs=pl.BlockSpec((rows, d), lambda s, f, te, tr: (s, 0)),
            scratch_shapes=[
                pltpu.VMEM((rows, d), F32), pltpu.VMEM((rows, d), BF16),
                pltpu.VMEM((d, tf), BF16), pltpu.VMEM((d, tf), BF16), pltpu.VMEM((tf, d), BF16),
                pltpu.SemaphoreType.DMA(()),
            ],
        ),
        out_shape=jax.ShapeDtypeStruct((n_tiles * rows, d), F32),
        compiler_params=_cparams(("arbitrary", "arbitrary")),
        name="routed_experts",
    )(tile_expert, tile_rows, src_tok, u, w_up, w_up,
      b_up.reshape(n_exp, 1, two_ff), b_up.reshape(n_exp, 1, two_ff), w_down, b_down.reshape(n_exp, 1, d))


def routing_tables(top_idx, n_tokens, n_experts, rows):
    n_assign = n_tokens * TOP_K
    n_tiles = n_assign // rows + n_experts
    e_flat = top_idx[:n_tokens, :TOP_K].reshape(-1)
    onehot = (e_flat[:, None] == jnp.arange(n_experts, dtype=I32)[None, :]).astype(I32)
    csum = jnp.cumsum(onehot, axis=0)
    count = csum[-1]
    pos = jnp.sum(csum * onehot, axis=1) - 1
    tiles_of = (count + rows - 1) // rows
    tile_end = jnp.cumsum(tiles_of)
    tile_start = tile_end - tiles_of
    used = tile_end[-1]
    dest = tile_start[e_flat] * rows + pos
    tile_id = jnp.arange(n_tiles, dtype=I32)
    last = used - 1
    tile_block = jnp.minimum(tile_id, last)
    tile_expert = jnp.sum((tile_end[None, :] <= tile_block[:, None]).astype(I32), axis=1)
    tile_rows = jnp.clip(count[tile_expert] - (tile_id - tile_start[tile_expert]) * rows, 0, rows)
    tile_rows = jnp.where(tile_id < used, tile_rows, 0)
    token_of = jnp.arange(n_assign, dtype=I32) // TOP_K
    src_tok = jnp.zeros((n_tiles * rows,), I32).at[dest].set(token_of).reshape(n_tiles, 1, rows)
    return tile_expert.astype(I32), tile_rows.astype(I32), src_tok, dest.astype(I32)


def _final_kernel(dst_ref, dnext_ref, y_hbm, h_ref, tw_ref, p_ref, gple_ref, wgate_ref, wproj_ref, gfin_ref, o_ref,
                  buf, sems, *, tm, n_tiles):
    i = pl.program_id(0)
    slot = lax.rem(i, 2)
    rows_per_trip = DMA_UNROLL // TOP_K

    def fetch(table_ref, to_slot):
        def issue(gi, carry):
            for u in range(rows_per_trip):
                r = gi * rows_per_trip + u
                for k in range(TOP_K):
                    _row_copy(y_hbm, buf.at[to_slot, k], table_ref[0, 0, k * tm + r], r, sems.at[to_slot]).start()
            return carry

        lax.fori_loop(0, tm // rows_per_trip, issue, 0)

    @pl.when(i == 0)
    def _():
        fetch(dst_ref, 0)

    @pl.when(i + 1 < n_tiles)
    def _():
        fetch(dnext_ref, 1 - slot)

    def drain(gi, carry):
        for _ in range(DMA_UNROLL):
            _row_copy(y_hbm, buf.at[slot, 0], 0, 0, sems.at[slot]).wait()
        return carry

    lax.fori_loop(0, TOP_K * tm // DMA_UNROLL, drain, 0)
    tw = tw_ref[...]
    h2 = h_ref[...]
    for k in range(TOP_K):
        h2 = h2 + tw[:, k:k + 1] * buf[slot, k]
    un = _rms(h2, gple_ref[...]).astype(BF16)
    gate = _sigmoid(_dot(un, wgate_ref[...]))
    h3 = h2 + gate * _dot(p_ref[...].astype(BF16), wproj_ref[...])
    o_ref[...] = _rms(h3, gfin_ref[...])


def combine_ple_final(y_rows, dest, h1, top_w, p, g_ple, w_gate, w_proj, g_final):
    m, d = h1.shape
    tm = ROW_TILE
    pd = p.shape[1]
    row = lambda i: (i, 0)
    fixed = lambda i: (0, 0)
    n_tiles = m // tm
    kern = functools.partial(_final_kernel, tm=tm, n_tiles=n_tiles)
    return pl.pallas_call(
        kern,
        grid=(n_tiles,),
        in_specs=[
            pl.BlockSpec((1, 1, TOP_K * tm), lambda i: (i, 0, 0), memory_space=pltpu.SMEM),
            pl.BlockSpec((1, 1, TOP_K * tm), lambda i: (jnp.minimum(i + 1, n_tiles - 1), 0, 0),
                         memory_space=pltpu.SMEM),
            pl.BlockSpec(memory_space=pl.ANY),
            pl.BlockSpec((tm, d), row), pl.BlockSpec((tm, LANES), row), pl.BlockSpec((tm, pd), row),
            pl.BlockSpec((1, d), fixed), pl.BlockSpec((d, d), fixed), pl.BlockSpec((pd, d), fixed),
            pl.BlockSpec((1, d), fixed),
        ],
        out_specs=pl.BlockSpec((tm, d), row),
        out_shape=jax.ShapeDtypeStruct((m, d), F32),
        scratch_shapes=[pltpu.VMEM((2, TOP_K, tm, d), F32), pltpu.SemaphoreType.DMA((2,))],
        compiler_params=_cparams(("arbitrary",)),
        name="combine_ple_final",
    )(dest, dest, y_rows, h1, top_w, p, g_ple.reshape(1, d), w_gate, w_proj, g_final.reshape(1, d))


def kernel(x_prompt, x_sample, cache_k, cache_v, page_table, state_conv, state_ssm, p_prompt, p_sample, g_mix, w_in, b_sb, w_conv, a_log, dt_bias, g_onorm, w_o_sb, w_o_gdn, w_out, g_ffn, w_router, b_router, w_up, b_up, w_down, b_down, g_ple, w_ple_gate, w_ple_proj, g_final):
    bp, sp, d = x_prompt.shape
    bs = x_sample.shape[0]
    assert x_sample.shape[1] == 1 and g_mix.shape[0] == 1
    n_prompt = bp * sp
    n_tok = n_prompt + bs
    tp = -(-n_tok // MM_ROW_TILE) * MM_ROW_TILE
    width = N_HEADS * HEAD_DIM
    conv_w = 3 * width
    n_experts = w_router.shape[2]

    rest_lo = 3 * width
    ba_lo = rest_lo + conv_w + width
    ba_hi = ba_lo + 2 * N_HEADS
    w_in_b = w_in[0].astype(BF16)
    w_rest = jnp.concatenate([w_in_b[:, rest_lo:ba_lo], w_in_b[:, ba_hi:]], axis=1)
    w_ba = jnp.zeros((d, LANES), BF16).at[:, :2 * N_HEADS].set(w_in_b[:, ba_lo:ba_hi])

    x_all = jnp.concatenate([x_prompt.reshape(n_prompt, d), x_sample.reshape(bs, d),
                             jnp.zeros((tp - n_tok, d), x_prompt.dtype)], axis=0)
    xn = rmsnorm_cast(x_all, g_mix[0], ROW_TILE)
    tm = MM_ROW_TILE
    q_b = matmul(xn, w_in_b[:, :width], tm, MM_COL_TILE, "in_proj_q", (BF16,), 0, n_prompt)
    k_f, k_b = matmul(xn, w_in_b[:, width:2 * width], tm, MM_COL_TILE, "in_proj_k", (F32, BF16), 0, n_prompt)
    v_f, v_b = matmul(xn, w_in_b[:, 2 * width:rest_lo], tm, MM_COL_TILE, "in_proj_v", (F32, BF16), 0, n_prompt)
    qkv_s = matmul(xn, w_in_b[:, :rest_lo], tm, MM_COL_TILE, "in_proj_qkv_sample", (F32,), n_prompt, tp - n_prompt)
    rest = matmul(xn, w_rest, tm, 2 * MM_COL_TILE, "in_proj_rest")
    ba = matmul(xn, w_ba, tm, LANES, "in_proj_gates")

    heads = lambda t, b, s: t.reshape(1, b, s, N_HEADS, HEAD_DIM)
    k_prompt = heads(k_f, bp, sp)
    v_prompt = heads(v_f, bp, sp)
    k_sample = heads(qkv_s[:bs, width:2 * width], bs, 1)
    v_sample = heads(qkv_s[:bs, 2 * width:rest_lo], bs, 1)

    o_sb_p = sb_attention_prompt(q_b, k_b, v_b, b_sb[0].astype(F32), bp, sp)
    n_pool, page = cache_k.shape[1], cache_k.shape[2]
    o_sb_s = sb_attention_decode(qkv_s[:bs, :width], b_sb[0].astype(F32),
                                 cache_k.reshape(n_pool, page * N_HEADS, HEAD_DIM),
                                 cache_v.reshape(n_pool, page * N_HEADS, HEAD_DIM), page_table)
    o_sb = jnp.concatenate([o_sb_p, o_sb_s.astype(BF16), jnp.zeros((tp - n_tok, width), BF16)], axis=0)

    z_blk = conv_w // width
    og_p, ssm_p = gated_deltanet(
        rest, 0, rest, z_blk, ba, jnp.zeros((bp, CONV_TAPS - 1, conv_w), F32),
        jnp.zeros((bp, N_HEADS, HEAD_DIM, HEAD_DIM), F32), w_conv[0], a_log[0], dt_bias[0], g_onorm[0],
        batch=bp, seq=sp, chunk=GDN_CHUNK, valid_rows=GDN_CHUNK)
    pad_rows = GDN_DECODE_ROWS
    spread = lambda t: jnp.zeros((bs, pad_rows, t.shape[1]), F32).at[:, 0].set(t).reshape(bs * pad_rows, t.shape[1])
    rest_s = rest[n_prompt:n_tok]
    conv_in_s = rest_s[:, :conv_w]
    og_s, ssm_s = gated_deltanet(
        spread(conv_in_s), 0, spread(rest_s[:, conv_w:conv_w + width]), 0, spread(ba[n_prompt:n_tok]),
        state_conv[0].astype(F32), state_ssm[0].astype(F32), w_conv[0], a_log[0], dt_bias[0], g_onorm[0],
        batch=bs, seq=pad_rows, chunk=pad_rows, valid_rows=1)
    o_g = jnp.concatenate([og_p, og_s.reshape(bs, pad_rows, width)[:, 0], jnp.zeros((tp - n_tok, width), BF16)], axis=0)
    tail = CONV_TAPS - 1
    conv_prompt = jnp.stack([rest[(b + 1) * sp - tail:(b + 1) * sp, :conv_w] for b in range(bp)])[None]
    conv_sample = jnp.concatenate([state_conv[0][:, 1:].astype(F32), conv_in_s[:, None]], axis=1)[None]

    gate_blk = (conv_w + width) // MM_COL_TILE
    merged = gated_merge(o_sb, o_g, rest, gate_blk, gate_blk + d // MM_COL_TILE,
                         w_o_sb[0].astype(BF16), w_o_gdn[0].astype(BF16))
    h1, u2, top_i, top_w = outproj_router(x_all, merged, w_out[0].astype(BF16), g_ffn[0], w_router[0], b_router[0])

    te, tr, src_tok, dest = routing_tables(top_i, n_tok, n_experts, MOE_TILE_ROWS)
    y_rows = routed_experts(u2, te, tr, src_tok, w_up[0], b_up[0], w_down[0], b_down[0])

    dest_tok = jnp.zeros((tp, TOP_K), I32).at[:n_tok].set(dest.reshape(n_tok, TOP_K))
    dest_tiles = dest_tok.reshape(tp // ROW_TILE, ROW_TILE, TOP_K).transpose(0, 2, 1).reshape(tp // ROW_TILE, 1, TOP_K * ROW_TILE)
    p_all = jnp.concatenate([p_prompt[0].reshape(n_prompt, -1), p_sample[0].reshape(bs, -1),
                             jnp.zeros((tp - n_tok, p_prompt.shape[-1]), p_prompt.dtype)], axis=0)
    y = combine_ple_final(y_rows, dest_tiles, h1, top_w, p_all, g_ple[0], w_ple_gate[0].astype(BF16),
                          w_ple_proj[0].astype(BF16), g_final)

    y_prompt = y[:n_prompt].reshape(bp, sp, d)
    y_sample = y[n_prompt:n_tok].reshape(bs, 1, d)
    return (y_prompt, y_sample, k_prompt, v_prompt, k_sample, v_sample,
            conv_prompt, ssm_p[None], conv_sample, ssm_s[None])
```

```python
import functools

import jax
import jax.numpy as jnp
from jax import lax
from jax.experimental import pallas as pl
from jax.experimental.pallas import tpu as pltpu

F32 = jnp.float32
BF16 = jnp.bfloat16
I32 = jnp.int32

EPS = 1e-6
LANES = 128
HEAD_DIM = 128
N_HEADS = 8
CONV_TAPS = 4
GDN_CHUNK = 64
GDN_DECODE_ROWS = 16
TOP_K = 4
SWIGLU_ALPHA = 1.702
SWIGLU_LIMIT = 7.0
VMEM_LIMIT = 56 * 1024 * 1024

ROW_TILE = 256
MM_ROW_TILE = 512
MM_COL_TILE = 1024
ATT_TILE = 256
ATT_HEAD_GROUP = 4
DECODE_PAGE_GROUP = 8
MOE_TILE_ROWS = 1280
MOE_ROW_STEPS = (512, 1024, 1280)
MOE_FF_TILE = 256
DMA_UNROLL = 8


def _cparams(sem):
    return pltpu.CompilerParams(dimension_semantics=sem, vmem_limit_bytes=VMEM_LIMIT)


def _sigmoid(x):
    return 1.0 / (1.0 + jnp.exp(-x))


def _softplus_neg_abs(x):
    return jnp.log(1.0 + jnp.exp(-jnp.abs(x)))


def _split2(x):
    hi = x.astype(BF16)
    lo = (x - hi.astype(F32)).astype(BF16)
    return hi, lo


def _split3(x):
    hi = x.astype(BF16)
    r = x - hi.astype(F32)
    mid = r.astype(BF16)
    lo = (r - mid.astype(F32)).astype(BF16)
    return hi, mid, lo


def _dot(a, b):
    return jnp.dot(a, b, preferred_element_type=F32)


def _dot_nt(a, b):
    return lax.dot_general(a, b, (((1,), (1,)), ((), ())), preferred_element_type=F32)


def _dot_tn(a, b):
    return lax.dot_general(a, b, (((0,), (0,)), ((), ())), preferred_element_type=F32)


def _dot_wide(a, b):
    ah, al = _split2(a)
    bh, bl = _split2(b)
    return _dot(ah, bh) + (_dot(ah, bl) + _dot(al, bh))


def _dot_exact_lhs(a_bf16, x):
    hi, mid, lo = _split3(x)
    return _dot(a_bf16, hi) + (_dot(a_bf16, mid) + _dot(a_bf16, lo))


def _rms(x, g):
    return x * lax.rsqrt(jnp.mean(x * x, axis=-1, keepdims=True) + EPS) * g


def _two_source_specs(tm, d, n_a):
    return [pl.BlockSpec((tm, d), lambda i: (jnp.minimum(i, n_a - 1), 0)),
            pl.BlockSpec((tm, d), lambda i: (jnp.maximum(i - n_a, 0), 0))]


def _two_source_tile(xa_ref, xb_ref, n_a):
    return jnp.where(pl.program_id(0) < n_a, xa_ref[...], xb_ref[...])


def _rms_kernel(xa_ref, xb_ref, g_ref, o_ref, *, n_a):
    o_ref[...] = _rms(_two_source_tile(xa_ref, xb_ref, n_a), g_ref[...]).astype(o_ref.dtype)


def rmsnorm_cast(xa, xb, g, tm):
    d = xa.shape[1]
    n_a, n_b = xa.shape[0] // tm, xb.shape[0] // tm
    return pl.pallas_call(
        functools.partial(_rms_kernel, n_a=n_a),
        grid=(n_a + n_b,),
        in_specs=_two_source_specs(tm, d, n_a) + [pl.BlockSpec((1, d), lambda i: (0, 0))],
        out_specs=pl.BlockSpec((tm, d), lambda i: (i, 0)),
        out_shape=jax.ShapeDtypeStruct(((n_a + n_b) * tm, d), BF16),
        compiler_params=_cparams(("arbitrary",)),
        name="rmsnorm_cast",
    )(xa, xb, g.reshape(1, d))


def _mm_kernel(a_ref, w_ref, *o_refs):
    acc = _dot(a_ref[...], w_ref[...])
    for o_ref in o_refs:
        o_ref[...] = acc.astype(o_ref.dtype)


def matmul(a, w, tm, tn, name, out_dtypes=(F32,), row_start=0, rows=None):
    k = a.shape[1]
    n = w.shape[1]
    rows = a.shape[0] if rows is None else rows
    first = row_start // tm
    outs = pl.pallas_call(
        _mm_kernel,
        grid=(n // tn, rows // tm),
        in_specs=[pl.BlockSpec((tm, k), lambda j, i: (first + i, 0)), pl.BlockSpec((k, tn), lambda j, i: (0, j))],
        out_specs=[pl.BlockSpec((tm, tn), lambda j, i: (i, j)) for _ in out_dtypes],
        out_shape=[jax.ShapeDtypeStruct((rows, n), dt) for dt in out_dtypes],
        compiler_params=_cparams(("arbitrary", "arbitrary")),
        name=name,
    )(a, w)
    return outs[0] if len(outs) == 1 else outs


def _suffix_matrix():
    j = lax.broadcasted_iota(I32, (LANES, 2 * LANES), 0)
    s = lax.broadcasted_iota(I32, (LANES, 2 * LANES), 1)
    return jnp.where((s >= LANES) | (j > s), 1.0, 0.0).astype(BF16)


def _sb_prompt_kernel(bias_ref, q_ref, k_ref, v_ref, cum_ref, o_ref, *, tile, scale, group):
    hg = pl.program_id(1)
    i = pl.program_id(2)
    heads = range(group)
    col = lambda h: slice(h * HEAD_DIM, (h + 1) * HEAD_DIM)
    q = [q_ref[:, col(h)] for h in heads]
    bias = [bias_ref[hg * group + h] for h in heads]
    cum = cum_ref[...]
    rows = lax.broadcasted_iota(I32, (tile, LANES), 0)
    cols = lax.broadcasted_iota(I32, (tile, LANES), 1)
    nsub = tile // LANES

    def block(j, carry, diagonal):
        run, acc = carry
        k0 = pl.multiple_of(j * tile, tile)
        s = [_dot_nt(q[h], k_ref[pl.ds(k0, tile), col(h)]) * scale + bias[h] for h in heads]
        for sb in reversed(range(nsub)):
            sub = [s[h][:, sb * LANES:(sb + 1) * LANES] for h in heads]
            t = [_softplus_neg_abs(x) for x in sub]
            log_beta = [jnp.minimum(x, 0.0) - tt for x, tt in zip(sub, t)]
            log_keep = [lb - x for lb, x in zip(log_beta, sub)]
            if diagonal:
                vis = (cols + sb * LANES) < rows
                log_keep = [jnp.where(vis, x, 0.0) for x in log_keep]
            c = [_dot(x.astype(BF16), cum) for x in log_keep]
            w = [jnp.exp(log_beta[h] + (c[h][:, :LANES] + run[h])) for h in heads]
            if diagonal:
                w = [jnp.where(vis, x, 0.0) for x in w]
            run = [run[h] + c[h][:, LANES:] for h in heads]
            ks = pl.multiple_of(k0 + sb * LANES, LANES)
            acc = [acc[h] + _dot(w[h].astype(BF16), v_ref[pl.ds(ks, LANES), col(h)]) for h in heads]
        return run, acc

    zero = [jnp.zeros((tile, LANES), F32) for _ in heads]
    carry = block(i, (zero, zero), True)
    _, acc = lax.fori_loop(1, i + 1, lambda jj, c: block(i - jj, c, False), carry)
    for h in heads:
        o_ref[:, col(h)] = acc[h].astype(o_ref.dtype)


def sb_attention_prompt(q, k, v, bias, batch, seq):
    tile = ATT_TILE
    group = ATT_HEAD_GROUP
    nq = seq // tile
    gw = group * HEAD_DIM
    kern = functools.partial(_sb_prompt_kernel, tile=tile, scale=HEAD_DIM ** -0.5, group=group)
    return pl.pallas_call(
        kern,
        grid=(batch, N_HEADS // group, nq),
        in_specs=[
            pl.BlockSpec(memory_space=pltpu.SMEM),
            pl.BlockSpec((tile, gw), lambda b, g, i: (b * nq + i, g)),
            pl.BlockSpec((seq, gw), lambda b, g, i: (b, g)),
            pl.BlockSpec((seq, gw), lambda b, g, i: (b, g)),
            pl.BlockSpec((LANES, 2 * LANES), lambda b, g, i: (0, 0)),
        ],
        out_specs=pl.BlockSpec((tile, gw), lambda b, g, i: (b * nq + i, g)),
        out_shape=jax.ShapeDtypeStruct((batch * seq, N_HEADS * HEAD_DIM), BF16),
        compiler_params=_cparams(("arbitrary", "arbitrary", "arbitrary")),
        name="sb_attention_prompt",
    )(bias, q, k, v, _suffix_matrix())


def _sb_decode_kernel(pt_ref, qt_ref, bias_ref, *refs, scale, n_steps, group, page):
    k_refs, v_refs = refs[:group], refs[group:2 * group]
    cumt_ref, expand_ref, o_ref, z_scr, run_scr, acc_scr = refs[2 * group:]
    jj = pl.program_id(1)

    @pl.when(jj == 0)
    def _():
        run_scr[...] = jnp.zeros_like(run_scr)
        acc_scr[...] = jnp.zeros_like(acc_scr)

    pages = range(group)
    lane = lax.broadcasted_iota(I32, (page, LANES), 1)
    qt = qt_ref[0]
    bias = bias_ref[...]
    for g in pages:
        z_scr[g] = _dot(k_refs[g][0].astype(BF16), qt)
    s = []
    for g in pages:
        acc = jnp.zeros((page, LANES), F32)
        for h in range(N_HEADS):
            acc = jnp.where(lane == h, z_scr[g, pl.ds(h, page, stride=N_HEADS), :], acc)
        s.append(acc * scale + bias)
    t = [_softplus_neg_abs(x) for x in s]
    log_beta = [jnp.minimum(x, 0.0) - tt for x, tt in zip(s, t)]
    log_keep = [lb - x for lb, x in zip(log_beta, s)]
    c = [_dot(cumt_ref[...], x.astype(BF16)) for x in log_keep]
    run = run_scr[...]
    w = []
    for g in pages:
        w.append(jnp.exp(log_beta[g] + (c[g][:page] + run)))
        run = run + c[g][page:]
    run_scr[...] = run
    wide = [_dot(x.astype(BF16), expand_ref[...]) for x in w]
    for h in range(N_HEADS):
        tot = acc_scr[h]
        for g in pages:
            vh = v_refs[g][0, pl.ds(h, page, stride=N_HEADS), :]
            prod = wide[g][:, h * HEAD_DIM:(h + 1) * HEAD_DIM] * vh
            tot = tot + jnp.sum(prod.reshape(page // 8, 8, HEAD_DIM), axis=0)
        acc_scr[h] = tot

    @pl.when(jj == n_steps - 1)
    def _():
        o_ref[0] = jnp.concatenate([jnp.sum(acc_scr[h], axis=0, keepdims=True) for h in range(N_HEADS)], axis=0)


def sb_attention_decode(q, bias, cache_k, cache_v, page_table):
    b, width = q.shape
    n_pages = page_table.shape[1]
    page = cache_k.shape[1] // N_HEADS
    group = DECODE_PAGE_GROUP
    assert page == LANES and n_pages % group == 0 and cache_k.shape[2] == HEAD_DIM
    n_steps = n_pages // group
    qt = jnp.zeros((b, HEAD_DIM, LANES), F32).at[:, :, :N_HEADS].set(
        q.reshape(b, N_HEADS, HEAD_DIM).transpose(0, 2, 1)).astype(BF16)
    bias_row = jnp.zeros((1, LANES), F32).at[0, :N_HEADS].set(bias)
    r = lax.broadcasted_iota(I32, (2 * page, page), 0)
    cidx = lax.broadcasted_iota(I32, (2 * page, page), 1)
    cumt = jnp.where((r >= page) | (cidx > r), 1.0, 0.0).astype(BF16)
    lane = lax.broadcasted_iota(I32, (LANES, width), 0)
    col_head = lax.broadcasted_iota(I32, (LANES, width), 1) // HEAD_DIM
    expand = jnp.where(lane == col_head, 1.0, 0.0).astype(BF16)

    def page_spec(g):
        return pl.BlockSpec((1, page * N_HEADS, HEAD_DIM),
                            lambda i, j, pt: (pt[i, n_pages - 1 - (j * group + g)], 0, 0))

    kern = functools.partial(_sb_decode_kernel, scale=HEAD_DIM ** -0.5, n_steps=n_steps, group=group, page=page)
    out = pl.pallas_call(
        kern,
        grid_spec=pltpu.PrefetchScalarGridSpec(
            num_scalar_prefetch=1,
            grid=(b, n_steps),
            in_specs=[pl.BlockSpec((1, HEAD_DIM, LANES), lambda i, j, pt: (i, 0, 0)),
                      pl.BlockSpec((1, LANES), lambda i, j, pt: (0, 0))]
                     + [page_spec(g) for g in range(group)] * 2
                     + [pl.BlockSpec((2 * page, page), lambda i, j, pt: (0, 0)),
                        pl.BlockSpec((LANES, width), lambda i, j, pt: (0, 0))],
            out_specs=pl.BlockSpec((1, N_HEADS, HEAD_DIM), lambda i, j, pt: (i, 0, 0)),
            scratch_shapes=[pltpu.VMEM((group, page * N_HEADS, LANES), F32), pltpu.VMEM((page, LANES), F32),
                            pltpu.VMEM((N_HEADS, 8, HEAD_DIM), F32)],
        ),
        out_shape=jax.ShapeDtypeStruct((b, N_HEADS, HEAD_DIM), F32),
        compiler_params=_cparams(("arbitrary", "arbitrary")),
        name="sb_attention_decode",
    )(page_table, qt, bias_row, *([cache_k] * group), *([cache_v] * group), cumt, expand)
    return out.reshape(b, width)


def _inv_unit_lower(mats, size):
    rr = lax.broadcasted_iota(I32, mats[0].shape, 0)
    cc = lax.broadcasted_iota(I32, mats[0].shape, 1)
    eye = jnp.where(rr == cc, 1.0, 0.0)
    t = [eye - n for n in mats]
    p = list(mats)
    span = 2
    while span < size:
        p = [_dot_wide(x, x) for x in p]
        t = [x + _dot_wide(x, y) for x, y in zip(t, p)]
        span *= 2
    return t


def _gdn_kernel(xc_ref, z_ref, ba_ref, hist_ref, s0_ref, wconv_ref, arow_ref, dtrow_ref, gon_ref, lmat_ref,
                og_ref, sout_ref, xp_scr, s_scr, *, chunk, n_chunks, valid_rows):
    c = pl.program_id(1)
    qk_w = N_HEADS * HEAD_DIM

    @pl.when(c == 0)
    def _():
        xp_scr[5:8, :] = hist_ref[0]
        s_scr[...] = s0_ref[0]

    x = xc_ref[...]
    xp_scr[8:8 + chunk, :] = x
    wc = wconv_ref[...]
    conv = xp_scr[5:5 + chunk, :] * wc[0:1, :]
    conv = conv + xp_scr[6:6 + chunk, :] * wc[1:2, :]
    conv = conv + xp_scr[7:7 + chunk, :] * wc[2:3, :]
    conv = conv + x * wc[3:4, :]
    xp_scr[5:8, :] = x[chunk - 3:chunk, :]
    conv = conv * _sigmoid(conv)

    ba = ba_ref[...]
    beta_all = _sigmoid(ba)
    sp = ba + dtrow_ref[...]
    g_all = -arow_ref[...] * (jnp.maximum(sp, 0.0) + _softplus_neg_abs(sp))
    if valid_rows < chunk:
        live = lax.broadcasted_iota(I32, ba.shape, 0) < valid_rows
        beta_all = jnp.where(live, beta_all, 0.0)
        g_all = jnp.where(live, g_all, 0.0)
    stacked = _dot_exact_lhs(lmat_ref[...], g_all)
    gc_all = stacked[:chunk]
    gc_t = stacked[:LANES].T
    g_tot = stacked[LANES:]

    ii = lax.broadcasted_iota(I32, (chunk, chunk), 0)
    jj = lax.broadcasted_iota(I32, (chunk, chunk), 1)
    incl = ii >= jj
    strict = ii > jj
    gon = gon_ref[...]

    heads = range(N_HEADS)
    col = lambda h, part: slice(part * qk_w + h * HEAD_DIM, part * qk_w + (h + 1) * HEAD_DIM)
    l2n = lambda x: x * lax.rsqrt(jnp.sum(x * x, axis=-1, keepdims=True) + EPS)
    q = [l2n(conv[:, col(h, 0)]) * (HEAD_DIM ** -0.5) for h in heads]
    k = [l2n(conv[:, col(h, 1)]) for h in heads]
    v = [conv[:, col(h, 2)] for h in heads]
    gate_lane = lambda h: slice(N_HEADS + h, N_HEADS + h + 1)
    bcol = [beta_all[:, h:h + 1] for h in heads]
    gcol = [gc_all[:, gate_lane(h)] for h in heads]
    grow = [gc_t[gate_lane(h), :chunk] for h in heads]
    glast_c = [g_tot[:chunk, gate_lane(h)] for h in heads]
    glast_s = [g_tot[:, gate_lane(h)] for h in heads]
    decay = [jnp.where(incl, jnp.exp(jnp.where(incl, gcol[h] - grow[h], 0.0)), 0.0) for h in heads]
    kb = [k[h] * bcol[h] for h in heads]
    k_b16 = [x.astype(BF16) for x in k]
    n_mat = [jnp.where(strict, _dot_nt(kb[h].astype(BF16), k_b16[h]) * decay[h], 0.0) for h in heads]
    t_inv = _inv_unit_lower(n_mat, chunk)
    rhs = [jnp.concatenate([v[h] * bcol[h], kb[h] * jnp.exp(gcol[h])], axis=1) for h in heads]
    sol = [_dot_wide(t_inv[h], rhs[h]) for h in heads]
    qk = [_dot_nt(q[h].astype(BF16), k_b16[h]) * decay[h] for h in heads]
    q_dec = [(q[h] * jnp.exp(gcol[h])).astype(BF16) for h in heads]
    k_end = [(k[h] * jnp.exp(glast_c[h] - gcol[h])).astype(BF16) for h in heads]
    state = [s_scr[h] for h in heads]
    state_b = [x.astype(BF16) for x in state]
    v_new = [sol[h][:, :HEAD_DIM] - _dot(sol[h][:, HEAD_DIM:].astype(BF16), state_b[h]) for h in heads]
    v_new_b = [x.astype(BF16) for x in v_new]
    o = [_dot(q_dec[h], state_b[h]) + _dot(qk[h].astype(BF16), v_new_b[h]) for h in heads]
    new_state = [state[h] * jnp.exp(glast_s[h]) + _dot_tn(k_end[h], v_new_b[h]) for h in heads]
    for h in heads:
        s_scr[h] = new_state[h]
    for h in heads:
        zh = z_ref[:, col(h, 0)]
        og_ref[:, col(h, 0)] = (_rms(o[h], gon) * (zh * _sigmoid(zh))).astype(og_ref.dtype)

    @pl.when(c == n_chunks - 1)
    def _():
        sout_ref[0] = s_scr[...]


def gated_deltanet(xc, xc_col, zz, z_col, ba, hist, s0, w_conv, a_log, dt_bias, g_onorm, *,
                   batch, seq, chunk, valid_rows):
    n_chunks = seq // chunk
    width = N_HEADS * HEAD_DIM
    a_row = jnp.zeros((1, LANES), F32).at[0, N_HEADS:2 * N_HEADS].set(jnp.exp(a_log.astype(F32)))
    dt_row = jnp.zeros((1, LANES), F32).at[0, N_HEADS:2 * N_HEADS].set(dt_bias.astype(F32))
    r = lax.broadcasted_iota(I32, (2 * LANES, chunk), 0)
    t = lax.broadcasted_iota(I32, (2 * LANES, chunk), 1)
    lmat = jnp.where((r >= LANES) | (t <= r), 1.0, 0.0).astype(BF16)
    kern = functools.partial(_gdn_kernel, chunk=chunk, n_chunks=n_chunks, valid_rows=valid_rows)
    return pl.pallas_call(
        kern,
        grid=(batch, n_chunks),
        in_specs=[
            pl.BlockSpec((chunk, 3 * width), lambda b, c: (b * n_chunks + c, xc_col)),
            pl.BlockSpec((chunk, width), lambda b, c: (b * n_chunks + c, z_col)),
            pl.BlockSpec((chunk, LANES), lambda b, c: (b * n_chunks + c, 0)),
            pl.BlockSpec((1, CONV_TAPS - 1, 3 * width), lambda b, c: (b, 0, 0)),
            pl.BlockSpec((1, N_HEADS, HEAD_DIM, HEAD_DIM), lambda b, c: (b, 0, 0, 0)),
            pl.BlockSpec((CONV_TAPS, 3 * width), lambda b, c: (0, 0)),
            pl.BlockSpec((1, LANES), lambda b, c: (0, 0)),
            pl.BlockSpec((1, LANES), lambda b, c: (0, 0)),
            pl.BlockSpec((1, HEAD_DIM), lambda b, c: (0, 0)),
            pl.BlockSpec((2 * LANES, chunk), lambda b, c: (0, 0)),
        ],
        out_specs=[
            pl.BlockSpec((chunk, width), lambda b, c: (b * n_chunks + c, 0)),
            pl.BlockSpec((1, N_HEADS, HEAD_DIM, HEAD_DIM), lambda b, c: (b, 0, 0, 0)),
        ],
        out_shape=[
            jax.ShapeDtypeStruct((batch * seq, width), BF16),
            jax.ShapeDtypeStruct((batch, N_HEADS, HEAD_DIM, HEAD_DIM), F32),
        ],
        scratch_shapes=[pltpu.VMEM((8 + chunk, 3 * width), F32), pltpu.VMEM((N_HEADS, HEAD_DIM, HEAD_DIM), F32)],
        compiler_params=_cparams(("arbitrary", "arbitrary")),
        name="gated_deltanet",
    )(xc, zz, ba, hist, s0, w_conv, a_row, dt_row, g_onorm.reshape(1, HEAD_DIM), lmat)


def _merge_kernel(a_ref, b_ref, ga_ref, gb_ref, wa_ref, wb_ref, o_ref):
    ya = _dot(a_ref[...], wa_ref[...])
    yb = _dot(b_ref[...], wb_ref[...])
    o_ref[...] = (_sigmoid(ga_ref[...]) * ya + _sigmoid(gb_ref[...]) * yb).astype(o_ref.dtype)


def gated_merge(a, b, proj, ga_col, gb_col, wa, wb):
    m, k = a.shape
    n = wa.shape[1]
    tm, tn = MM_ROW_TILE, MM_COL_TILE
    return pl.pallas_call(
        _merge_kernel,
        grid=(n // tn, m // tm),
        in_specs=[
            pl.BlockSpec((tm, k), lambda j, i: (i, 0)),
            pl.BlockSpec((tm, k), lambda j, i: (i, 0)),
            pl.BlockSpec((tm, tn), lambda j, i: (i, ga_col + j)),
            pl.BlockSpec((tm, tn), lambda j, i: (i, gb_col + j)),
            pl.BlockSpec((k, tn), lambda j, i: (0, j)),
            pl.BlockSpec((k, tn), lambda j, i: (0, j)),
        ],
        out_specs=pl.BlockSpec((tm, tn), lambda j, i: (i, j)),
        out_shape=jax.ShapeDtypeStruct((m, n), BF16),
        compiler_params=_cparams(("arbitrary", "arbitrary")),
        name="gated_merge",
    )(a, b, proj, proj, wa, wb)


def _outproj_router_kernel(xa_ref, xb_ref, mg_ref, wout_ref, g_ref, wrh_ref, wrl_ref, br_ref,
                           h_ref, u_ref, ti_ref, tw_ref, *, n_experts, n_a):
    h1 = _two_source_tile(xa_ref, xb_ref, n_a) + _dot(mg_ref[...], wout_ref[...])
    h_ref[...] = h1
    u = _rms(h1, g_ref[...])
    u_ref[...] = u
    uh, ul = _split2(u)
    logits = _dot(uh, wrh_ref[...]) + (_dot(ul, wrh_ref[...]) + _dot(uh, wrl_ref[...])) + br_ref[...]
    lane = lax.broadcasted_iota(I32, logits.shape, 1).astype(F32)
    vals = jnp.where(lane < n_experts, logits, -jnp.inf)
    top_v, top_i = [], []
    for _ in range(TOP_K):
        m = jnp.max(vals, axis=-1, keepdims=True)
        idx = jnp.min(jnp.where(vals == m, lane, float(LANES)), axis=-1, keepdims=True)
        top_v.append(m)
        top_i.append(idx)
        vals = jnp.where(lane == idx, -jnp.inf, vals)
    ex = [jnp.exp(v - top_v[0]) for v in top_v]
    den = ex[0]
    for e in ex[1:]:
        den = den + e
    tw = jnp.zeros(logits.shape, F32)
    ti = jnp.zeros(logits.shape, F32)
    for k in range(TOP_K):
        tw = jnp.where(lane == k, ex[k] / den, tw)
        ti = jnp.where(lane == k, top_i[k], ti)
    tw_ref[...] = tw
    ti_ref[...] = ti.astype(I32)


def outproj_router(xa, xb, merged, w_out, g_ffn, w_router, b_router):
    m, d = merged.shape
    n_experts = w_router.shape[1]
    tm = ROW_TILE
    n_a = xa.shape[0] // tm
    wr = jnp.zeros((d, LANES), F32).at[:, :n_experts].set(w_router.astype(F32))
    wr_hi = wr.astype(BF16)
    wr_lo = (wr - wr_hi.astype(F32)).astype(BF16)
    br = jnp.zeros((1, LANES), F32).at[0, :n_experts].set(b_router.astype(F32))
    row = lambda i: (i, 0)
    fixed = lambda i: (0, 0)
    kern = functools.partial(_outproj_router_kernel, n_experts=n_experts, n_a=n_a)
    return pl.pallas_call(
        kern,
        grid=(m // tm,),
        in_specs=_two_source_specs(tm, d, n_a) + [
            pl.BlockSpec((tm, d), row), pl.BlockSpec((d, d), fixed),
            pl.BlockSpec((1, d), fixed), pl.BlockSpec((d, LANES), fixed), pl.BlockSpec((d, LANES), fixed),
            pl.BlockSpec((1, LANES), fixed),
        ],
        out_specs=[pl.BlockSpec((tm, d), row), pl.BlockSpec((tm, d), row),
                   pl.BlockSpec((tm, LANES), row), pl.BlockSpec((tm, LANES), row)],
        out_shape=[jax.ShapeDtypeStruct((m, d), F32), jax.ShapeDtypeStruct((m, d), F32),
                   jax.ShapeDtypeStruct((m, LANES), I32), jax.ShapeDtypeStruct((m, LANES), F32)],
        compiler_params=_cparams(("arbitrary",)),
        name="outproj_router",
    )(xa, xb, merged, w_out, g_ffn.reshape(1, d), wr_hi, wr_lo, br)


def _row_copy(src_hbm, dst, src_row, dst_row, sem):
    return pltpu.make_async_copy(src_hbm.at[pl.ds(src_row, 1)], dst.at[pl.ds(dst_row, 1)], sem)


def _moe_kernel(te_ref, tr_ref, idx_ref, u_hbm, wg_ref, wl_ref, bg_ref, bl_ref, wd_ref, bd_ref,
                y_ref, xg_scr, xb_scr, sem, *, row_steps):
    s = pl.program_id(0)
    f = pl.program_id(1)
    n_rows = tr_ref[s]

    @pl.when((s == 0) & (f == 0))
    def _():
        xg_scr[...] = jnp.zeros_like(xg_scr)

    @pl.when(n_rows > 0)
    def _():
        @pl.when(f == 0)
        def _():
            n_groups = (n_rows + DMA_UNROLL - 1) // DMA_UNROLL

            def issue(gi, carry):
                for k in range(DMA_UNROLL):
                    r = gi * DMA_UNROLL + k
                    _row_copy(u_hbm, xg_scr, idx_ref[0, 0, r], r, sem).start()
                return carry

            lax.fori_loop(0, n_groups, issue, 0)

            def drain(gi, carry):
                for k in range(DMA_UNROLL):
                    _row_copy(u_hbm, xg_scr, 0, 0, sem).wait()
                return carry

            lax.fori_loop(0, n_groups, drain, 0)
            xb_scr[...] = xg_scr[...].astype(BF16)
            y_ref[...] = jnp.broadcast_to(bd_ref[0], y_ref.shape)

    def expert_pass(m):
        xb = xb_scr[0:m, :]
        glu = jnp.minimum(_dot(xb, wg_ref[0].astype(BF16)) + bg_ref[0], SWIGLU_LIMIT)
        lin = jnp.clip(_dot(xb, wl_ref[0].astype(BF16)) + bl_ref[0], -SWIGLU_LIMIT, SWIGLU_LIMIT)
        act = glu * _sigmoid(SWIGLU_ALPHA * glu) * (lin + 1.0)
        y_ref[0:m, :] += _dot(act.astype(BF16), wd_ref[0].astype(BF16))

    lower = 0
    for m in row_steps:
        @pl.when((n_rows > lower) & (n_rows <= m))
        def _(m=m):
            expert_pass(m)
        lower = m

    @pl.when((n_rows == 0) & (f == 0))
    def _():
        y_ref[...] = jnp.zeros_like(y_ref)


def routed_experts(u, tile_expert, tile_rows, src_tok, w_up, b_up, w_down, b_down):
    n_tiles, _, rows = src_tok.shape
    n_exp, d, two_ff = w_up.shape
    d_ff = two_ff // 2
    tf = MOE_FF_TILE
    nf = d_ff // tf

    def ff(s, f, tr):
        return jnp.where(tr[s] > 0, f, nf - 1)

    assert MOE_ROW_STEPS[-1] == rows
    kern = functools.partial(_moe_kernel, row_steps=MOE_ROW_STEPS)
    return pl.pallas_call(
        kern,
        grid_spec=pltpu.PrefetchScalarGridSpec(
            num_scalar_prefetch=2,
            grid=(n_tiles, nf),
            in_specs=[
                pl.BlockSpec((1, 1, rows), lambda s, f, te, tr: (s, 0, 0), memory_space=pltpu.SMEM),
                pl.BlockSpec(memory_space=pl.ANY),
                pl.BlockSpec((1, d, tf), lambda s, f, te, tr: (te[s], 0, ff(s, f, tr))),
                pl.BlockSpec((1, d, tf), lambda s, f, te, tr: (te[s], 0, nf + ff(s, f, tr))),
                pl.BlockSpec((1, 1, tf), lambda s, f, te, tr: (te[s], 0, ff(s, f, tr))),
                pl.BlockSpec((1, 1, tf), lambda s, f, te, tr: (te[s], 0, nf + ff(s, f, tr))),
                pl.BlockSpec((1, tf, d), lambda s, f, te, tr: (te[s], ff(s, f, tr), 0)),
                pl.BlockSpec((1, 1, d), lambda s, f, te, tr: (te[s], 0, 0)),
            ],
            out_specs=pl.BlockSpec((rows, d), lambda s, f, te, tr: (s, 0)),
            scratch_shapes=[
                pltpu.VMEM((rows, d), F32), pltpu.VMEM((rows, d), BF16), pltpu.SemaphoreType.DMA(()),
            ],
        ),
        out_shape=jax.ShapeDtypeStruct((n_tiles * rows, d), F32),
        compiler_params=_cparams(("arbitrary", "arbitrary")),
        name="routed_experts",
    )(tile_expert, tile_rows, src_tok, u, w_up, w_up,
      b_up.reshape(n_exp, 1, two_ff), b_up.reshape(n_exp, 1, two_ff), w_down, b_down.reshape(n_exp, 1, d))


def routing_tables(top_idx, n_tokens, n_experts, rows):
    n_assign = n_tokens * TOP_K
    n_tiles = n_assign // rows + n_experts
    e_flat = top_idx[:n_tokens, :TOP_K].reshape(-1)
    onehot = (e_flat[:, None] == jnp.arange(n_experts, dtype=I32)[None, :]).astype(I32)
    csum = jnp.cumsum(onehot, axis=0)
    count = csum[-1]
    pos = jnp.sum(csum * onehot, axis=1) - 1
    tiles_of = (count + rows - 1) // rows
    tile_end = jnp.cumsum(tiles_of)
    tile_start = tile_end - tiles_of
    used = tile_end[-1]
    dest = tile_start[e_flat] * rows + pos
    tile_id = jnp.arange(n_tiles, dtype=I32)
    last = used - 1
    tile_block = jnp.minimum(tile_id, last)
    tile_expert = jnp.sum((tile_end[None, :] <= tile_block[:, None]).astype(I32), axis=1)
    tile_rows = jnp.clip(count[tile_expert] - (tile_id - tile_start[tile_expert]) * rows, 0, rows)
    tile_rows = jnp.where(tile_id < used, tile_rows, 0)
    token_of = jnp.arange(n_assign, dtype=I32) // TOP_K
    src_tok = jnp.zeros((n_tiles * rows,), I32).at[dest].set(token_of).reshape(n_tiles, 1, rows)
    return tile_expert.astype(I32), tile_rows.astype(I32), src_tok, dest.astype(I32)


def _final_kernel(dst_ref, dnext_ref, y_hbm, h_ref, tw_ref, p_ref, gple_ref, wgate_ref, wproj_ref, gfin_ref, o_ref,
                  buf, sems, *, tm, n_tiles):
    i = pl.program_id(0)
    slot = lax.rem(i, 2)
    rows_per_trip = DMA_UNROLL // TOP_K

    def fetch(table_ref, to_slot):
        def issue(gi, carry):
            for u in range(rows_per_trip):
                r = gi * rows_per_trip + u
                for k in range(TOP_K):
                    _row_copy(y_hbm, buf.at[to_slot, k], table_ref[0, 0, k * tm + r], r,
                              sems.at[to_slot]).start(priority=k % 2)
            return carry

        lax.fori_loop(0, tm // rows_per_trip, issue, 0)

    @pl.when(i == 0)
    def _():
        fetch(dst_ref, 0)

    @pl.when(i + 1 < n_tiles)
    def _():
        fetch(dnext_ref, 1 - slot)

    def drain(gi, carry):
        for _ in range(DMA_UNROLL):
            _row_copy(y_hbm, buf.at[slot, 0], 0, 0, sems.at[slot]).wait()
        return carry

    lax.fori_loop(0, TOP_K * tm // DMA_UNROLL, drain, 0)
    tw = tw_ref[...]
    h2 = h_ref[...]
    for k in range(TOP_K):
        h2 = h2 + tw[:, k:k + 1] * buf[slot, k]
    un = _rms(h2, gple_ref[...]).astype(BF16)
    gate = _sigmoid(_dot(un, wgate_ref[...]))
    h3 = h2 + gate * _dot(p_ref[...].astype(BF16), wproj_ref[...])
    o_ref[...] = _rms(h3, gfin_ref[...])


def combine_ple_final(y_rows, dest, h1, top_w, p, g_ple, w_gate, w_proj, g_final):
    m, d = h1.shape
    tm = ROW_TILE
    pd = p.shape[1]
    row = lambda i: (i, 0)
    fixed = lambda i: (0, 0)
    n_tiles = m // tm
    kern = functools.partial(_final_kernel, tm=tm, n_tiles=n_tiles)
    return pl.pallas_call(
        kern,
        grid=(n_tiles,),
        in_specs=[
            pl.BlockSpec((1, 1, TOP_K * tm), lambda i: (i, 0, 0), memory_space=pltpu.SMEM),
            pl.BlockSpec((1, 1, TOP_K * tm), lambda i: (jnp.minimum(i + 1, n_tiles - 1), 0, 0),
                         memory_space=pltpu.SMEM),
            pl.BlockSpec(memory_space=pl.ANY),
            pl.BlockSpec((tm, d), row), pl.BlockSpec((tm, LANES), row), pl.BlockSpec((tm, pd), row),
            pl.BlockSpec((1, d), fixed), pl.BlockSpec((d, d), fixed), pl.BlockSpec((pd, d), fixed),
            pl.BlockSpec((1, d), fixed),
        ],
        out_specs=pl.BlockSpec((tm, d), row),
        out_shape=jax.ShapeDtypeStruct((m, d), F32),
        scratch_shapes=[pltpu.VMEM((2, TOP_K, tm, d), F32), pltpu.SemaphoreType.DMA((2,))],
        compiler_params=_cparams(("arbitrary",)),
        name="combine_ple_final",
    )(dest, dest, y_rows, h1, top_w, p, g_ple.reshape(1, d), w_gate, w_proj, g_final.reshape(1, d))


def kernel(x_prompt, x_sample, cache_k, cache_v, page_table, state_conv, state_ssm, p_prompt, p_sample, g_mix, w_in, b_sb, w_conv, a_log, dt_bias, g_onorm, w_o_sb, w_o_gdn, w_out, g_ffn, w_router, b_router, w_up, b_up, w_down, b_down, g_ple, w_ple_gate, w_ple_proj, g_final):
    bp, sp, d = x_prompt.shape
    bs = x_sample.shape[0]
    assert x_sample.shape[1] == 1 and g_mix.shape[0] == 1
    n_prompt = bp * sp
    n_tok = n_prompt + bs
    tp = -(-n_tok // MM_ROW_TILE) * MM_ROW_TILE
    width = N_HEADS * HEAD_DIM
    conv_w = 3 * width
    n_experts = w_router.shape[2]

    rest_lo = 3 * width
    ba_lo = rest_lo + conv_w + width
    ba_hi = ba_lo + 2 * N_HEADS
    w_in_b = w_in[0].astype(BF16)
    w_rest = jnp.concatenate([w_in_b[:, rest_lo:ba_lo], w_in_b[:, ba_hi:]], axis=1)
    w_ba = jnp.zeros((d, LANES), BF16).at[:, :2 * N_HEADS].set(w_in_b[:, ba_lo:ba_hi])

    x_p = x_prompt.reshape(n_prompt, d)
    x_s = jnp.zeros((tp - n_prompt, d), x_sample.dtype).at[:bs].set(x_sample.reshape(bs, d))
    xn = rmsnorm_cast(x_p, x_s, g_mix[0], ROW_TILE)
    tm = MM_ROW_TILE
    q_b = matmul(xn, w_in_b[:, :width], tm, MM_COL_TILE, "in_proj_q", (BF16,), 0, n_prompt)
    k_f, k_b = matmul(xn, w_in_b[:, width:2 * width], tm, MM_COL_TILE, "in_proj_k", (F32, BF16), 0, n_prompt)
    v_f, v_b = matmul(xn, w_in_b[:, 2 * width:rest_lo], tm, MM_COL_TILE, "in_proj_v", (F32, BF16), 0, n_prompt)
    qkv_s = matmul(xn, w_in_b[:, :rest_lo], tm, MM_COL_TILE, "in_proj_qkv_sample", (F32,), n_prompt, tp - n_prompt)
    rest = matmul(xn, w_rest, tm, 2 * MM_COL_TILE, "in_proj_rest")
    ba = matmul(xn, w_ba, tm, LANES, "in_proj_gates")

    heads = lambda t, b, s: t.reshape(1, b, s, N_HEADS, HEAD_DIM)
    k_prompt = heads(k_f, bp, sp)
    v_prompt = heads(v_f, bp, sp)
    k_sample = heads(qkv_s[:bs, width:2 * width], bs, 1)
    v_sample = heads(qkv_s[:bs, 2 * width:rest_lo], bs, 1)

    o_sb_p = sb_attention_prompt(q_b, k_b, v_b, b_sb[0].astype(F32), bp, sp)
    n_pool, page = cache_k.shape[1], cache_k.shape[2]
    o_sb_s = sb_attention_decode(qkv_s[:bs, :width], b_sb[0].astype(F32),
                                 cache_k.reshape(n_pool, page * N_HEADS, HEAD_DIM),
                                 cache_v.reshape(n_pool, page * N_HEADS, HEAD_DIM), page_table)
    o_sb = jnp.concatenate([o_sb_p, o_sb_s.astype(BF16), jnp.zeros((tp - n_tok, width), BF16)], axis=0)

    z_blk = conv_w // width
    og_p, ssm_p = gated_deltanet(
        rest, 0, rest, z_blk, ba, jnp.zeros((bp, CONV_TAPS - 1, conv_w), F32),
        jnp.zeros((bp, N_HEADS, HEAD_DIM, HEAD_DIM), F32), w_conv[0], a_log[0], dt_bias[0], g_onorm[0],
        batch=bp, seq=sp, chunk=GDN_CHUNK, valid_rows=GDN_CHUNK)
    pad_rows = GDN_DECODE_ROWS
    spread = lambda t: jnp.zeros((bs, pad_rows, t.shape[1]), F32).at[:, 0].set(t).reshape(bs * pad_rows, t.shape[1])
    rest_s = rest[n_prompt:n_tok]
    conv_in_s = rest_s[:, :conv_w]
    og_s, ssm_s = gated_deltanet(
        spread(conv_in_s), 0, spread(rest_s[:, conv_w:conv_w + width]), 0, spread(ba[n_prompt:n_tok]),
        state_conv[0].astype(F32), state_ssm[0].astype(F32), w_conv[0], a_log[0], dt_bias[0], g_onorm[0],
        batch=bs, seq=pad_rows, chunk=pad_rows, valid_rows=1)
    o_g = jnp.concatenate([og_p, og_s.reshape(bs, pad_rows, width)[:, 0], jnp.zeros((tp - n_tok, width), BF16)], axis=0)
    tail = CONV_TAPS - 1
    conv_prompt = jnp.stack([rest[(b + 1) * sp - tail:(b + 1) * sp, :conv_w] for b in range(bp)])[None]
    conv_sample = jnp.concatenate([state_conv[0][:, 1:].astype(F32), conv_in_s[:, None]], axis=1)[None]

    gate_blk = (conv_w + width) // MM_COL_TILE
    merged = gated_merge(o_sb, o_g, rest, gate_blk, gate_blk + d // MM_COL_TILE,
                         w_o_sb[0].astype(BF16), w_o_gdn[0].astype(BF16))
    h1, u2, top_i, top_w = outproj_router(x_p, x_s, merged, w_out[0].astype(BF16), g_ffn[0], w_router[0], b_router[0])

    te, tr, src_tok, dest = routing_tables(top_i, n_tok, n_experts, MOE_TILE_ROWS)
    y_rows = routed_experts(u2, te, tr, src_tok, w_up[0], b_up[0], w_down[0], b_down[0])

    dest_tok = jnp.zeros((tp, TOP_K), I32).at[:n_tok].set(dest.reshape(n_tok, TOP_K))
    dest_tiles = dest_tok.reshape(tp // ROW_TILE, ROW_TILE, TOP_K).transpose(0, 2, 1).reshape(tp // ROW_TILE, 1, TOP_K * ROW_TILE)
    p_all = jnp.concatenate([p_prompt[0].reshape(n_prompt, -1), p_sample[0].reshape(bs, -1),
                             jnp.zeros((tp - n_tok, p_prompt.shape[-1]), p_prompt.dtype)], axis=0)
    y = combine_ple_final(y_rows, dest_tiles, h1, top_w, p_all, g_ple[0], w_ple_gate[0].astype(BF16),
                          w_ple_proj[0].astype(BF16), g_final)

    y_prompt = y[:n_prompt].reshape(bp, sp, d)
    y_sample = y[n_prompt:n_tok].reshape(bs, 1, d)
    return (y_prompt, y_sample, k_prompt, v_prompt, k_sample, v_sample,
            conv_prompt, ssm_p[None], conv_sample, ssm_s[None])
```

```python
import functools

import jax
import jax.numpy as jnp
from jax import lax
from jax.experimental import pallas as pl
from jax.experimental.pallas import tpu as pltpu

F32 = jnp.float32
BF16 = jnp.bfloat16
I32 = jnp.int32

EPS = 1e-6
LANES = 128
HEAD_DIM = 128
N_HEADS = 8
CONV_TAPS = 4
GDN_CHUNK = 64
GDN_DECODE_ROWS = 16
TOP_K = 4
SWIGLU_ALPHA = 1.702
SWIGLU_LIMIT = 7.0
VMEM_LIMIT = 56 * 1024 * 1024

ROW_TILE = 256
MM_ROW_TILE = 512
MM_COL_TILE = 1024
ATT_TILE = 512
ATT_HEAD_GROUP = 4
DECODE_PAGE_GROUP = 8
MOE_TILE_ROWS = 1280
MOE_ROW_STEPS = (512, 1024, 1280)
MOE_FF_TILE = 256
DMA_UNROLL = 8


def _cparams(sem):
    return pltpu.CompilerParams(dimension_semantics=sem, vmem_limit_bytes=VMEM_LIMIT)


def _sigmoid(x):
    return 1.0 / (1.0 + jnp.exp(-x))


def _softplus_neg_abs(x):
    return jnp.log(1.0 + jnp.exp(-jnp.abs(x)))


def _split2(x):
    hi = x.astype(BF16)
    lo = (x - hi.astype(F32)).astype(BF16)
    return hi, lo


def _split3(x):
    hi = x.astype(BF16)
    r = x - hi.astype(F32)
    mid = r.astype(BF16)
    lo = (r - mid.astype(F32)).astype(BF16)
    return hi, mid, lo


def _dot(a, b):
    return jnp.dot(a, b, preferred_element_type=F32)


def _dot_nt(a, b):
    return lax.dot_general(a, b, (((1,), (1,)), ((), ())), preferred_element_type=F32)


def _dot_tn(a, b):
    return lax.dot_general(a, b, (((0,), (0,)), ((), ())), preferred_element_type=F32)


def _dot_wide(a, b):
    ah, al = _split2(a)
    bh, bl = _split2(b)
    return _dot(ah, bh) + (_dot(ah, bl) + _dot(al, bh))


def _dot_exact_lhs(a_bf16, x):
    hi, mid, lo = _split3(x)
    return _dot(a_bf16, hi) + (_dot(a_bf16, mid) + _dot(a_bf16, lo))


def _rms(x, g):
    return x * lax.rsqrt(jnp.mean(x * x, axis=-1, keepdims=True) + EPS) * g


def _two_source_specs(tm, d, n_a):
    return [pl.BlockSpec((tm, d), lambda i: (jnp.minimum(i, n_a - 1), 0)),
            pl.BlockSpec((tm, d), lambda i: (jnp.maximum(i - n_a, 0), 0))]


def _two_source_tile(xa_ref, xb_ref, n_a):
    return jnp.where(pl.program_id(0) < n_a, xa_ref[...], xb_ref[...])


def _rms_kernel(xa_ref, xb_ref, g_ref, o_ref, *, n_a):
    o_ref[...] = _rms(_two_source_tile(xa_ref, xb_ref, n_a), g_ref[...]).astype(o_ref.dtype)


def rmsnorm_cast(xa, xb, g, tm):
    d = xa.shape[1]
    n_a, n_b = xa.shape[0] // tm, xb.shape[0] // tm
    return pl.pallas_call(
        functools.partial(_rms_kernel, n_a=n_a),
        grid=(n_a + n_b,),
        in_specs=_two_source_specs(tm, d, n_a) + [pl.BlockSpec((1, d), lambda i: (0, 0))],
        out_specs=pl.BlockSpec((tm, d), lambda i: (i, 0)),
        out_shape=jax.ShapeDtypeStruct(((n_a + n_b) * tm, d), BF16),
        compiler_params=_cparams(("arbitrary",)),
        name="rmsnorm_cast",
    )(xa, xb, g.reshape(1, d))


def _mm_kernel(a_ref, w_ref, *o_refs):
    acc = _dot(a_ref[...], w_ref[...])
    for o_ref in o_refs:
        o_ref[...] = acc.astype(o_ref.dtype)


def matmul(a, w, tm, tn, name, out_dtypes=(F32,), row_start=0, rows=None, col_start=0, cols=None):
    k = a.shape[1]
    n = w.shape[1] if cols is None else cols
    rows = a.shape[0] if rows is None else rows
    first = row_start // tm
    first_col = col_start // tn
    outs = pl.pallas_call(
        _mm_kernel,
        grid=(n // tn, rows // tm),
        in_specs=[pl.BlockSpec((tm, k), lambda j, i: (first + i, 0)),
                  pl.BlockSpec((k, tn), lambda j, i: (0, first_col + j))],
        out_specs=[pl.BlockSpec((tm, tn), lambda j, i: (i, j)) for _ in out_dtypes],
        out_shape=[jax.ShapeDtypeStruct((rows, n), dt) for dt in out_dtypes],
        compiler_params=_cparams(("arbitrary", "arbitrary")),
        name=name,
    )(a, w)
    return outs[0] if len(outs) == 1 else outs


def _suffix_matrix():
    j = lax.broadcasted_iota(I32, (LANES, 2 * LANES), 0)
    s = lax.broadcasted_iota(I32, (LANES, 2 * LANES), 1)
    return jnp.where((s >= LANES) | (j > s), 1.0, 0.0).astype(BF16)


def _sb_prompt_kernel(bias_ref, q_ref, k_ref, v_ref, cum_ref, o_ref, *, tile, scale, group):
    hg = pl.program_id(1)
    i = pl.program_id(2)
    heads = range(group)
    col = lambda h: slice(h * HEAD_DIM, (h + 1) * HEAD_DIM)
    q = [q_ref[:, col(h)] for h in heads]
    bias = [bias_ref[hg * group + h] for h in heads]
    cum = cum_ref[...]
    rows = lax.broadcasted_iota(I32, (tile, LANES), 0)
    cols = lax.broadcasted_iota(I32, (tile, LANES), 1)
    nsub = tile // LANES

    def block(j, carry, diagonal):
        run, acc = carry
        k0 = pl.multiple_of(j * tile, tile)
        s = [_dot_nt(q[h], k_ref[pl.ds(k0, tile), col(h)]) * scale + bias[h] for h in heads]
        for sb in reversed(range(nsub)):
            sub = [s[h][:, sb * LANES:(sb + 1) * LANES] for h in heads]
            t = [_softplus_neg_abs(x) for x in sub]
            log_beta = [jnp.minimum(x, 0.0) - tt for x, tt in zip(sub, t)]
            log_keep = [lb - x for lb, x in zip(log_beta, sub)]
            if diagonal:
                vis = (cols + sb * LANES) < rows
                log_keep = [jnp.where(vis, x, 0.0) for x in log_keep]
            c = [_dot(x.astype(BF16), cum) for x in log_keep]
            w = [jnp.exp(log_beta[h] + (c[h][:, :LANES] + run[h])) for h in heads]
            if diagonal:
                w = [jnp.where(vis, x, 0.0) for x in w]
            run = [run[h] + c[h][:, LANES:] for h in heads]
            ks = pl.multiple_of(k0 + sb * LANES, LANES)
            acc = [acc[h] + _dot(w[h].astype(BF16), v_ref[pl.ds(ks, LANES), col(h)]) for h in heads]
        return run, acc

    zero = [jnp.zeros((tile, LANES), F32) for _ in heads]
    carry = block(i, (zero, zero), True)
    _, acc = lax.fori_loop(1, i + 1, lambda jj, c: block(i - jj, c, False), carry)
    for h in heads:
        o_ref[:, col(h)] = acc[h].astype(o_ref.dtype)


def sb_attention_prompt(q, k, v, bias, batch, seq):
    tile = ATT_TILE
    group = ATT_HEAD_GROUP
    nq = seq // tile
    gw = group * HEAD_DIM
    kern = functools.partial(_sb_prompt_kernel, tile=tile, scale=HEAD_DIM ** -0.5, group=group)
    return pl.pallas_call(
        kern,
        grid=(batch, N_HEADS // group, nq),
        in_specs=[
            pl.BlockSpec(memory_space=pltpu.SMEM),
            pl.BlockSpec((tile, gw), lambda b, g, i: (b * nq + i, g)),
            pl.BlockSpec((seq, gw), lambda b, g, i: (b, g)),
            pl.BlockSpec((seq, gw), lambda b, g, i: (b, g)),
            pl.BlockSpec((LANES, 2 * LANES), lambda b, g, i: (0, 0)),
        ],
        out_specs=pl.BlockSpec((tile, gw), lambda b, g, i: (b * nq + i, g)),
        out_shape=jax.ShapeDtypeStruct((batch * seq, N_HEADS * HEAD_DIM), BF16),
        compiler_params=_cparams(("arbitrary", "arbitrary", "arbitrary")),
        name="sb_attention_prompt",
    )(bias, q, k, v, _suffix_matrix())


def _sb_decode_kernel(pt_ref, qt_ref, bias_ref, *refs, scale, n_steps, group, page):
    k_refs, v_refs = refs[:group], refs[group:2 * group]
    cumt_ref, expand_ref, o_ref, z_scr, run_scr, acc_scr = refs[2 * group:]
    jj = pl.program_id(1)

    @pl.when(jj == 0)
    def _():
        run_scr[...] = jnp.zeros_like(run_scr)
        acc_scr[...] = jnp.zeros_like(acc_scr)

    pages = range(group)
    lane = lax.broadcasted_iota(I32, (page, LANES), 1)
    qt = qt_ref[0]
    bias = bias_ref[...]
    for g in pages:
        z_scr[g] = _dot(k_refs[g][0].astype(BF16), qt)
    s = []
    for g in pages:
        acc = jnp.zeros((page, LANES), F32)
        for h in range(N_HEADS):
            acc = jnp.where(lane == h, z_scr[g, pl.ds(h, page, stride=N_HEADS), :], acc)
        s.append(acc * scale + bias)
    t = [_softplus_neg_abs(x) for x in s]
    log_beta = [jnp.minimum(x, 0.0) - tt for x, tt in zip(s, t)]
    log_keep = [lb - x for lb, x in zip(log_beta, s)]
    c = [_dot(cumt_ref[...], x.astype(BF16)) for x in log_keep]
    run = run_scr[...]
    w = []
    for g in pages:
        w.append(jnp.exp(log_beta[g] + (c[g][:page] + run)))
        run = run + c[g][page:]
    run_scr[...] = run
    wide = [_dot(x.astype(BF16), expand_ref[...]) for x in w]
    for h in range(N_HEADS):
        tot = acc_scr[h]
        for g in pages:
            vh = v_refs[g][0, pl.ds(h, page, stride=N_HEADS), :]
            prod = wide[g][:, h * HEAD_DIM:(h + 1) * HEAD_DIM] * vh
            tot = tot + jnp.sum(prod.reshape(page // 8, 8, HEAD_DIM), axis=0)
        acc_scr[h] = tot

    @pl.when(jj == n_steps - 1)
    def _():
        o_ref[0] = jnp.concatenate([jnp.sum(acc_scr[h], axis=0, keepdims=True) for h in range(N_HEADS)], axis=0)


def sb_attention_decode(q, bias, cache_k, cache_v, page_table):
    b, width = q.shape
    n_pages = page_table.shape[1]
    page = cache_k.shape[1] // N_HEADS
    group = DECODE_PAGE_GROUP
    assert page == LANES and n_pages % group == 0 and cache_k.shape[2] == HEAD_DIM
    n_steps = n_pages // group
    qt = jnp.zeros((b, HEAD_DIM, LANES), F32).at[:, :, :N_HEADS].set(
        q.reshape(b, N_HEADS, HEAD_DIM).transpose(0, 2, 1)).astype(BF16)
    bias_row = jnp.zeros((1, LANES), F32).at[0, :N_HEADS].set(bias)
    r = lax.broadcasted_iota(I32, (2 * page, page), 0)
    cidx = lax.broadcasted_iota(I32, (2 * page, page), 1)
    cumt = jnp.where((r >= page) | (cidx > r), 1.0, 0.0).astype(BF16)
    lane = lax.broadcasted_iota(I32, (LANES, width), 0)
    col_head = lax.broadcasted_iota(I32, (LANES, width), 1) // HEAD_DIM
    expand = jnp.where(lane == col_head, 1.0, 0.0).astype(BF16)

    def page_spec(g):
        return pl.BlockSpec((1, page * N_HEADS, HEAD_DIM),
                            lambda i, j, pt: (pt[i, n_pages - 1 - (j * group + g)], 0, 0))

    kern = functools.partial(_sb_decode_kernel, scale=HEAD_DIM ** -0.5, n_steps=n_steps, group=group, page=page)
    out = pl.pallas_call(
        kern,
        grid_spec=pltpu.PrefetchScalarGridSpec(
            num_scalar_prefetch=1,
            grid=(b, n_steps),
            in_specs=[pl.BlockSpec((1, HEAD_DIM, LANES), lambda i, j, pt: (i, 0, 0)),
                      pl.BlockSpec((1, LANES), lambda i, j, pt: (0, 0))]
                     + [page_spec(g) for g in range(group)] * 2
                     + [pl.BlockSpec((2 * page, page), lambda i, j, pt: (0, 0)),
                        pl.BlockSpec((LANES, width), lambda i, j, pt: (0, 0))],
            out_specs=pl.BlockSpec((1, N_HEADS, HEAD_DIM), lambda i, j, pt: (i, 0, 0)),
            scratch_shapes=[pltpu.VMEM((group, page * N_HEADS, LANES), F32), pltpu.VMEM((page, LANES), F32),
                            pltpu.VMEM((N_HEADS, 8, HEAD_DIM), F32)],
        ),
        out_shape=jax.ShapeDtypeStruct((b, N_HEADS, HEAD_DIM), F32),
        compiler_params=_cparams(("arbitrary", "arbitrary")),
        name="sb_attention_decode",
    )(page_table, qt, bias_row, *([cache_k] * group), *([cache_v] * group), cumt, expand)
    return out.reshape(b, width)


def _inv_unit_lower(mats, size):
    rr = lax.broadcasted_iota(I32, mats[0].shape, 0)
    cc = lax.broadcasted_iota(I32, mats[0].shape, 1)
    eye = jnp.where(rr == cc, 1.0, 0.0)
    t = [eye - n for n in mats]
    p = list(mats)
    span = 2
    while span < size:
        p = [_dot_wide(x, x) for x in p]
        t = [x + _dot_wide(x, y) for x, y in zip(t, p)]
        span *= 2
    return t


def _gdn_kernel(xc_ref, z_ref, ba_ref, hist_ref, s0_ref, wconv_ref, arow_ref, dtrow_ref, gon_ref, lmat_ref,
                og_ref, sout_ref, xp_scr, s_scr, *, chunk, n_chunks, valid_rows):
    c = pl.program_id(1)
    qk_w = N_HEADS * HEAD_DIM

    @pl.when(c == 0)
    def _():
        xp_scr[5:8, :] = hist_ref[0]
        s_scr[...] = s0_ref[0]

    x = xc_ref[...]
    xp_scr[8:8 + chunk, :] = x
    wc = wconv_ref[...]
    conv = xp_scr[5:5 + chunk, :] * wc[0:1, :]
    conv = conv + xp_scr[6:6 + chunk, :] * wc[1:2, :]
    conv = conv + xp_scr[7:7 + chunk, :] * wc[2:3, :]
    conv = conv + x * wc[3:4, :]
    xp_scr[5:8, :] = x[chunk - 3:chunk, :]
    conv = conv * _sigmoid(conv)

    ba = ba_ref[...]
    beta_all = _sigmoid(ba)
    sp = ba + dtrow_ref[...]
    g_all = -arow_ref[...] * (jnp.maximum(sp, 0.0) + _softplus_neg_abs(sp))
    if valid_rows < chunk:
        live = lax.broadcasted_iota(I32, ba.shape, 0) < valid_rows
        beta_all = jnp.where(live, beta_all, 0.0)
        g_all = jnp.where(live, g_all, 0.0)
    stacked = _dot_exact_lhs(lmat_ref[...], g_all)
    gc_all = stacked[:chunk]
    gc_t = stacked[:LANES].T
    g_tot = stacked[LANES:]

    ii = lax.broadcasted_iota(I32, (chunk, chunk), 0)
    jj = lax.broadcasted_iota(I32, (chunk, chunk), 1)
    incl = ii >= jj
    strict = ii > jj
    gon = gon_ref[...]

    heads = range(N_HEADS)
    col = lambda h, part: slice(part * qk_w + h * HEAD_DIM, part * qk_w + (h + 1) * HEAD_DIM)
    l2n = lambda x: x * lax.rsqrt(jnp.sum(x * x, axis=-1, keepdims=True) + EPS)
    q = [l2n(conv[:, col(h, 0)]) * (HEAD_DIM ** -0.5) for h in heads]
    k = [l2n(conv[:, col(h, 1)]) for h in heads]
    v = [conv[:, col(h, 2)] for h in heads]
    gate_lane = lambda h: slice(N_HEADS + h, N_HEADS + h + 1)
    bcol = [beta_all[:, h:h + 1] for h in heads]
    gcol = [gc_all[:, gate_lane(h)] for h in heads]
    grow = [gc_t[gate_lane(h), :chunk] for h in heads]
    glast_c = [g_tot[:chunk, gate_lane(h)] for h in heads]
    glast_s = [g_tot[:, gate_lane(h)] for h in heads]
    decay = [jnp.where(incl, jnp.exp(jnp.where(incl, gcol[h] - grow[h], 0.0)), 0.0) for h in heads]
    kb = [k[h] * bcol[h] for h in heads]
    k_b16 = [x.astype(BF16) for x in k]
    n_mat = [jnp.where(strict, _dot_nt(kb[h].astype(BF16), k_b16[h]) * decay[h], 0.0) for h in heads]
    t_inv = _inv_unit_lower(n_mat, chunk)
    rhs = [jnp.concatenate([v[h] * bcol[h], kb[h] * jnp.exp(gcol[h])], axis=1) for h in heads]
    sol = [_dot_wide(t_inv[h], rhs[h]) for h in heads]
    qk = [_dot_nt(q[h].astype(BF16), k_b16[h]) * decay[h] for h in heads]
    q_dec = [(q[h] * jnp.exp(gcol[h])).astype(BF16) for h in heads]
    k_end = [(k[h] * jnp.exp(glast_c[h] - gcol[h])).astype(BF16) for h in heads]
    state = [s_scr[h] for h in heads]
    state_b = [x.astype(BF16) for x in state]
    v_new = [sol[h][:, :HEAD_DIM] - _dot(sol[h][:, HEAD_DIM:].astype(BF16), state_b[h]) for h in heads]
    v_new_b = [x.astype(BF16) for x in v_new]
    o = [_dot(q_dec[h], state_b[h]) + _dot(qk[h].astype(BF16), v_new_b[h]) for h in heads]
    new_state = [state[h] * jnp.exp(glast_s[h]) + _dot_tn(k_end[h], v_new_b[h]) for h in heads]
    for h in heads:
        s_scr[h] = new_state[h]
    for h in heads:
        zh = z_ref[:, col(h, 0)]
        og_ref[:, col(h, 0)] = (_rms(o[h], gon) * (zh * _sigmoid(zh))).astype(og_ref.dtype)

    @pl.when(c == n_chunks - 1)
    def _():
        sout_ref[0] = s_scr[...]


def gated_deltanet(xc, xc_col, zz, z_col, ba, hist, s0, w_conv, a_log, dt_bias, g_onorm, *,
                   batch, seq, chunk, valid_rows):
    n_chunks = seq // chunk
    width = N_HEADS * HEAD_DIM
    a_row = jnp.zeros((1, LANES), F32).at[0, N_HEADS:2 * N_HEADS].set(jnp.exp(a_log.astype(F32)))
    dt_row = jnp.zeros((1, LANES), F32).at[0, N_HEADS:2 * N_HEADS].set(dt_bias.astype(F32))
    r = lax.broadcasted_iota(I32, (2 * LANES, chunk), 0)
    t = lax.broadcasted_iota(I32, (2 * LANES, chunk), 1)
    lmat = jnp.where((r >= LANES) | (t <= r), 1.0, 0.0).astype(BF16)
    kern = functools.partial(_gdn_kernel, chunk=chunk, n_chunks=n_chunks, valid_rows=valid_rows)
    return pl.pallas_call(
        kern,
        grid=(batch, n_chunks),
        in_specs=[
            pl.BlockSpec((chunk, 3 * width), lambda b, c: (b * n_chunks + c, xc_col)),
            pl.BlockSpec((chunk, width), lambda b, c: (b * n_chunks + c, z_col)),
            pl.BlockSpec((chunk, LANES), lambda b, c: (b * n_chunks + c, 0)),
            pl.BlockSpec((1, CONV_TAPS - 1, 3 * width), lambda b, c: (b, 0, 0)),
            pl.BlockSpec((1, N_HEADS, HEAD_DIM, HEAD_DIM), lambda b, c: (b, 0, 0, 0)),
            pl.BlockSpec((CONV_TAPS, 3 * width), lambda b, c: (0, 0)),
            pl.BlockSpec((1, LANES), lambda b, c: (0, 0)),
            pl.BlockSpec((1, LANES), lambda b, c: (0, 0)),
            pl.BlockSpec((1, HEAD_DIM), lambda b, c: (0, 0)),
            pl.BlockSpec((2 * LANES, chunk), lambda b, c: (0, 0)),
        ],
        out_specs=[
            pl.BlockSpec((chunk, width), lambda b, c: (b * n_chunks + c, 0)),
            pl.BlockSpec((1, N_HEADS, HEAD_DIM, HEAD_DIM), lambda b, c: (b, 0, 0, 0)),
        ],
        out_shape=[
            jax.ShapeDtypeStruct((batch * seq, width), BF16),
            jax.ShapeDtypeStruct((batch, N_HEADS, HEAD_DIM, HEAD_DIM), F32),
        ],
        scratch_shapes=[pltpu.VMEM((8 + chunk, 3 * width), F32), pltpu.VMEM((N_HEADS, HEAD_DIM, HEAD_DIM), F32)],
        compiler_params=_cparams(("arbitrary", "arbitrary")),
        name="gated_deltanet",
    )(xc, zz, ba, hist, s0, w_conv, a_row, dt_row, g_onorm.reshape(1, HEAD_DIM), lmat)


def _merge_kernel(a_ref, b_ref, ga_ref, gb_ref, wa_ref, wb_ref, o_ref):
    ya = _dot(a_ref[...], wa_ref[...])
    yb = _dot(b_ref[...], wb_ref[...])
    o_ref[...] = (_sigmoid(ga_ref[...]) * ya + _sigmoid(gb_ref[...]) * yb).astype(o_ref.dtype)


def gated_merge(a, b, proj, ga_col, gb_col, wa, wb):
    m, k = a.shape
    n = wa.shape[1]
    tm, tn = MM_ROW_TILE, MM_COL_TILE
    return pl.pallas_call(
        _merge_kernel,
        grid=(n // tn, m // tm),
        in_specs=[
            pl.BlockSpec((tm, k), lambda j, i: (i, 0)),
            pl.BlockSpec((tm, k), lambda j, i: (i, 0)),
            pl.BlockSpec((tm, tn), lambda j, i: (i, ga_col + j)),
            pl.BlockSpec((tm, tn), lambda j, i: (i, gb_col + j)),
            pl.BlockSpec((k, tn), lambda j, i: (0, j)),
            pl.BlockSpec((k, tn), lambda j, i: (0, j)),
        ],
        out_specs=pl.BlockSpec((tm, tn), lambda j, i: (i, j)),
        out_shape=jax.ShapeDtypeStruct((m, n), BF16),
        compiler_params=_cparams(("arbitrary", "arbitrary")),
        name="gated_merge",
    )(a, b, proj, proj, wa, wb)


def _outproj_router_kernel(xa_ref, xb_ref, mg_ref, wout_ref, g_ref, wrh_ref, wrl_ref, br_ref,
                           h_ref, u_ref, ti_ref, tw_ref, *, n_experts, n_a):
    h1 = _two_source_tile(xa_ref, xb_ref, n_a) + _dot(mg_ref[...], wout_ref[...])
    h_ref[...] = h1
    u = _rms(h1, g_ref[...])
    u_ref[...] = u
    uh, ul = _split2(u)
    logits = _dot(uh, wrh_ref[...]) + (_dot(ul, wrh_ref[...]) + _dot(uh, wrl_ref[...])) + br_ref[...]
    lane = lax.broadcasted_iota(I32, logits.shape, 1).astype(F32)
    vals = jnp.where(lane < n_experts, logits, -jnp.inf)
    top_v, top_i = [], []
    for _ in range(TOP_K):
        m = jnp.max(vals, axis=-1, keepdims=True)
        idx = jnp.min(jnp.where(vals == m, lane, float(LANES)), axis=-1, keepdims=True)
        top_v.append(m)
        top_i.append(idx)
        vals = jnp.where(lane == idx, -jnp.inf, vals)
    ex = [jnp.exp(v - top_v[0]) for v in top_v]
    den = ex[0]
    for e in ex[1:]:
        den = den + e
    tw = jnp.zeros(logits.shape, F32)
    ti = jnp.zeros(logits.shape, F32)
    for k in range(TOP_K):
        tw = jnp.where(lane == k, ex[k] / den, tw)
        ti = jnp.where(lane == k, top_i[k], ti)
    tw_ref[...] = tw
    ti_ref[...] = ti.astype(I32)


def outproj_router(xa, xb, merged, w_out, g_ffn, w_router, b_router):
    m, d = merged.shape
    n_experts = w_router.shape[1]
    tm = ROW_TILE
    n_a = xa.shape[0] // tm
    wr = jnp.zeros((d, LANES), F32).at[:, :n_experts].set(w_router.astype(F32))
    wr_hi = wr.astype(BF16)
    wr_lo = (wr - wr_hi.astype(F32)).astype(BF16)
    br = jnp.zeros((1, LANES), F32).at[0, :n_experts].set(b_router.astype(F32))
    row = lambda i: (i, 0)
    fixed = lambda i: (0, 0)
    kern = functools.partial(_outproj_router_kernel, n_experts=n_experts, n_a=n_a)
    return pl.pallas_call(
        kern,
        grid=(m // tm,),
        in_specs=_two_source_specs(tm, d, n_a) + [
            pl.BlockSpec((tm, d), row), pl.BlockSpec((d, d), fixed),
            pl.BlockSpec((1, d), fixed), pl.BlockSpec((d, LANES), fixed), pl.BlockSpec((d, LANES), fixed),
            pl.BlockSpec((1, LANES), fixed),
        ],
        out_specs=[pl.BlockSpec((tm, d), row), pl.BlockSpec((tm, d), row),
                   pl.BlockSpec((tm, LANES), row), pl.BlockSpec((tm, LANES), row)],
        out_shape=[jax.ShapeDtypeStruct((m, d), F32), jax.ShapeDtypeStruct((m, d), F32),
                   jax.ShapeDtypeStruct((m, LANES), I32), jax.ShapeDtypeStruct((m, LANES), F32)],
        compiler_params=_cparams(("arbitrary",)),
        name="outproj_router",
    )(xa, xb, merged, w_out, g_ffn.reshape(1, d), wr_hi, wr_lo, br)


def _row_copy(src_hbm, dst, src_row, dst_row, sem):
    return pltpu.make_async_copy(src_hbm.at[pl.ds(src_row, 1)], dst.at[pl.ds(dst_row, 1)], sem)


def _moe_kernel(te_ref, tr_ref, idx_ref, u_hbm, wg_ref, wl_ref, bg_ref, bl_ref, wd_ref, bd_ref,
                y_ref, xg_scr, xb_scr, sem, *, row_steps):
    s = pl.program_id(0)
    f = pl.program_id(1)
    n_rows = tr_ref[s]

    @pl.when((s == 0) & (f == 0))
    def _():
        xg_scr[...] = jnp.zeros_like(xg_scr)

    @pl.when(n_rows > 0)
    def _():
        @pl.when(f == 0)
        def _():
            n_groups = (n_rows + DMA_UNROLL - 1) // DMA_UNROLL

            def issue(gi, carry):
                for k in range(DMA_UNROLL):
                    r = gi * DMA_UNROLL + k
                    _row_copy(u_hbm, xg_scr, idx_ref[0, 0, r], r, sem).start()
                return carry

            lax.fori_loop(0, n_groups, issue, 0)

            def drain(gi, carry):
                for k in range(DMA_UNROLL):
                    _row_copy(u_hbm, xg_scr, 0, 0, sem).wait()
                return carry

            lax.fori_loop(0, n_groups, drain, 0)
            xb_scr[...] = xg_scr[...].astype(BF16)
            y_ref[...] = jnp.broadcast_to(bd_ref[0], y_ref.shape)

    def expert_pass(m):
        xb = xb_scr[0:m, :]
        glu = jnp.minimum(_dot(xb, wg_ref[0].astype(BF16)) + bg_ref[0], SWIGLU_LIMIT)
        lin = jnp.clip(_dot(xb, wl_ref[0].astype(BF16)) + bl_ref[0], -SWIGLU_LIMIT, SWIGLU_LIMIT)
        act = glu * _sigmoid(SWIGLU_ALPHA * glu) * (lin + 1.0)
        y_ref[0:m, :] += _dot(act.astype(BF16), wd_ref[0].astype(BF16))

    lower = 0
    for m in row_steps:
        @pl.when((n_rows > lower) & (n_rows <= m))
        def _(m=m):
            expert_pass(m)
        lower = m

    @pl.when((n_rows == 0) & (f == 0))
    def _():
        y_ref[...] = jnp.zeros_like(y_ref)


def routed_experts(u, tile_expert, tile_rows, src_tok, w_up, b_up, w_down, b_down):
    n_tiles, _, rows = src_tok.shape
    n_exp, d, two_ff = w_up.shape
    d_ff = two_ff // 2
    tf = MOE_FF_TILE
    nf = d_ff // tf

    def ff(s, f, tr):
        return jnp.where(tr[s] > 0, f, nf - 1)

    assert MOE_ROW_STEPS[-1] == rows
    kern = functools.partial(_moe_kernel, row_steps=MOE_ROW_STEPS)
    return pl.pallas_call(
        kern,
        grid_spec=pltpu.PrefetchScalarGridSpec(
            num_scalar_prefetch=2,
            grid=(n_tiles, nf),
            in_specs=[
                pl.BlockSpec((1, 1, rows), lambda s, f, te, tr: (s, 0, 0), memory_space=pltpu.SMEM),
                pl.BlockSpec(memory_space=pl.ANY),
                pl.BlockSpec((1, d, tf), lambda s, f, te, tr: (te[s], 0, ff(s, f, tr))),
                pl.BlockSpec((1, d, tf), lambda s, f, te, tr: (te[s], 0, nf + ff(s, f, tr))),
                pl.BlockSpec((1, 1, tf), lambda s, f, te, tr: (te[s], 0, ff(s, f, tr))),
                pl.BlockSpec((1, 1, tf), lambda s, f, te, tr: (te[s], 0, nf + ff(s, f, tr))),
                pl.BlockSpec((1, tf, d), lambda s, f, te, tr: (te[s], ff(s, f, tr), 0)),
                pl.BlockSpec((1, 1, d), lambda s, f, te, tr: (te[s], 0, 0)),
            ],
            out_specs=pl.BlockSpec((rows, d), lambda s, f, te, tr: (s, 0)),
            scratch_shapes=[
                pltpu.VMEM((rows, d), F32), pltpu.VMEM((rows, d), BF16), pltpu.SemaphoreType.DMA(()),
            ],
        ),
        out_shape=jax.ShapeDtypeStruct((n_tiles * rows, d), F32),
        compiler_params=_cparams(("arbitrary", "arbitrary")),
        name="routed_experts",
    )(tile_expert, tile_rows, src_tok, u, w_up, w_up,
      b_up.reshape(n_exp, 1, two_ff), b_up.reshape(n_exp, 1, two_ff), w_down, b_down.reshape(n_exp, 1, d))


def routing_tables(top_idx, n_tokens, n_experts, rows):
    n_assign = n_tokens * TOP_K
    n_tiles = n_assign // rows + n_experts
    e_flat = top_idx[:n_tokens, :TOP_K].reshape(-1)
    onehot = (e_flat[:, None] == jnp.arange(n_experts, dtype=I32)[None, :]).astype(I32)
    csum = jnp.cumsum(onehot, axis=0)
    count = csum[-1]
    pos = jnp.sum(csum * onehot, axis=1) - 1
    tiles_of = (count + rows - 1) // rows
    tile_end = jnp.cumsum(tiles_of)
    tile_start = tile_end - tiles_of
    used = tile_end[-1]
    dest = tile_start[e_flat] * rows + pos
    tile_id = jnp.arange(n_tiles, dtype=I32)
    last = used - 1
    tile_block = jnp.minimum(tile_id, last)
    tile_expert = jnp.sum((tile_end[None, :] <= tile_block[:, None]).astype(I32), axis=1)
    tile_rows = jnp.clip(count[tile_expert] - (tile_id - tile_start[tile_expert]) * rows, 0, rows)
    tile_rows = jnp.where(tile_id < used, tile_rows, 0)
    token_of = jnp.arange(n_assign, dtype=I32) // TOP_K
    src_tok = jnp.zeros((n_tiles * rows,), I32).at[dest].set(token_of).reshape(n_tiles, 1, rows)
    return tile_expert.astype(I32), tile_rows.astype(I32), src_tok, dest.astype(I32)


def _final_kernel(dst_ref, dnext_ref, y_hbm, h_ref, tw_ref, p_ref, gple_ref, wgate_ref, wproj_ref, gfin_ref,
                  oa_ref, ob_ref, buf, sems, *, tm, n_tiles, n_a):
    i = pl.program_id(0)
    slot = lax.rem(i, 2)
    rows_per_trip = DMA_UNROLL // TOP_K

    def fetch(table_ref, to_slot):
        def issue(gi, carry):
            for u in range(rows_per_trip):
                r = gi * rows_per_trip + u
                for k in range(TOP_K):
                    _row_copy(y_hbm, buf.at[to_slot, k], table_ref[0, 0, k * tm + r], r, sems.at[to_slot]).start()
            return carry

        lax.fori_loop(0, tm // rows_per_trip, issue, 0)

    @pl.when(i == 0)
    def _():
        fetch(dst_ref, 0)

    @pl.when(i + 1 < n_tiles)
    def _():
        fetch(dnext_ref, 1 - slot)

    def drain(gi, carry):
        for _ in range(DMA_UNROLL):
            _row_copy(y_hbm, buf.at[slot, 0], 0, 0, sems.at[slot]).wait()
        return carry

    lax.fori_loop(0, TOP_K * tm // DMA_UNROLL, drain, 0)
    tw = tw_ref[...]
    h2 = h_ref[...]
    for k in range(TOP_K):
        h2 = h2 + tw[:, k:k + 1] * buf[slot, k]
    un = _rms(h2, gple_ref[...]).astype(BF16)
    gate = _sigmoid(_dot(un, wgate_ref[...]))
    h3 = h2 + gate * _dot(p_ref[...].astype(BF16), wproj_ref[...])
    y = _rms(h3, gfin_ref[...])

    @pl.when(i < n_a)
    def _():
        oa_ref[...] = y

    @pl.when(i >= n_a)
    def _():
        ob_ref[...] = y


def combine_ple_final(y_rows, dest, h1, top_w, p, g_ple, w_gate, w_proj, g_final, rows_a):
    m, d = h1.shape
    tm = ROW_TILE
    pd = p.shape[1]
    row = lambda i: (i, 0)
    fixed = lambda i: (0, 0)
    n_tiles = m // tm
    n_a = rows_a // tm
    kern = functools.partial(_final_kernel, tm=tm, n_tiles=n_tiles, n_a=n_a)
    return pl.pallas_call(
        kern,
        grid=(n_tiles,),
        in_specs=[
            pl.BlockSpec((1, 1, TOP_K * tm), lambda i: (i, 0, 0), memory_space=pltpu.SMEM),
            pl.BlockSpec((1, 1, TOP_K * tm), lambda i: (jnp.minimum(i + 1, n_tiles - 1), 0, 0),
                         memory_space=pltpu.SMEM),
            pl.BlockSpec(memory_space=pl.ANY),
            pl.BlockSpec((tm, d), row), pl.BlockSpec((tm, LANES), row), pl.BlockSpec((tm, pd), row),
            pl.BlockSpec((1, d), fixed), pl.BlockSpec((d, d), fixed), pl.BlockSpec((pd, d), fixed),
            pl.BlockSpec((1, d), fixed),
        ],
        out_specs=[pl.BlockSpec((tm, d), lambda i: (jnp.minimum(i, n_a - 1), 0)),
                   pl.BlockSpec((tm, d), lambda i: (jnp.maximum(i - n_a, 0), 0))],
        out_shape=[jax.ShapeDtypeStruct((n_a * tm, d), F32), jax.ShapeDtypeStruct(((n_tiles - n_a) * tm, d), F32)],
        scratch_shapes=[pltpu.VMEM((2, TOP_K, tm, d), F32), pltpu.SemaphoreType.DMA((2,))],
        compiler_params=_cparams(("arbitrary",)),
        name="combine_ple_final",
    )(dest, dest, y_rows, h1, top_w, p, g_ple.reshape(1, d), w_gate, w_proj, g_final.reshape(1, d))


def kernel(x_prompt, x_sample, cache_k, cache_v, page_table, state_conv, state_ssm, p_prompt, p_sample, g_mix, w_in, b_sb, w_conv, a_log, dt_bias, g_onorm, w_o_sb, w_o_gdn, w_out, g_ffn, w_router, b_router, w_up, b_up, w_down, b_down, g_ple, w_ple_gate, w_ple_proj, g_final):
    bp, sp, d = x_prompt.shape
    bs = x_sample.shape[0]
    assert x_sample.shape[1] == 1 and g_mix.shape[0] == 1
    n_prompt = bp * sp
    n_tok = n_prompt + bs
    tp = -(-n_tok // MM_ROW_TILE) * MM_ROW_TILE
    width = N_HEADS * HEAD_DIM
    conv_w = 3 * width
    n_experts = w_router.shape[2]

    rest_lo = 3 * width
    ba_lo = rest_lo + conv_w + width
    ba_hi = ba_lo + 2 * N_HEADS
    w_cols = lambda lo, hi: w_in[0, :, lo:hi].astype(BF16)
    w_rest = jnp.concatenate([w_cols(rest_lo, ba_lo), w_cols(ba_hi, w_in.shape[2])], axis=1)
    w_ba = jnp.zeros((d, LANES), BF16).at[:, :2 * N_HEADS].set(w_cols(ba_lo, ba_hi))

    x_p = x_prompt.reshape(n_prompt, d)
    x_s = jnp.zeros((tp - n_prompt, d), x_sample.dtype).at[:bs].set(x_sample.reshape(bs, d))
    xn = rmsnorm_cast(x_p, x_s, g_mix[0], ROW_TILE)
    tm = MM_ROW_TILE
    w_qkv = w_cols(0, rest_lo)
    q_b = matmul(xn, w_qkv, tm, MM_COL_TILE, "in_proj_q", (BF16,), 0, n_prompt, 0, width)
    k_f, k_b = matmul(xn, w_qkv, tm, MM_COL_TILE, "in_proj_k", (F32, BF16), 0, n_prompt, width, width)
    v_f, v_b = matmul(xn, w_qkv, tm, MM_COL_TILE, "in_proj_v", (F32, BF16), 0, n_prompt, 2 * width, width)
    qkv_s = matmul(xn, w_qkv, tm, MM_COL_TILE, "in_proj_qkv_sample", (F32,), n_prompt, tp - n_prompt)
    rest = matmul(xn, w_rest, tm, 2 * MM_COL_TILE, "in_proj_rest")
    ba = matmul(xn, w_ba, tm, LANES, "in_proj_gates")

    heads = lambda t, b, s: t.reshape(1, b, s, N_HEADS, HEAD_DIM)
    k_prompt = heads(k_f, bp, sp)
    v_prompt = heads(v_f, bp, sp)
    k_sample = heads(qkv_s[:bs, width:2 * width], bs, 1)
    v_sample = heads(qkv_s[:bs, 2 * width:rest_lo], bs, 1)

    o_sb_p = sb_attention_prompt(q_b, k_b, v_b, b_sb[0].astype(F32), bp, sp)
    n_pool, page = cache_k.shape[1], cache_k.shape[2]
    o_sb_s = sb_attention_decode(qkv_s[:bs, :width], b_sb[0].astype(F32),
                                 cache_k.reshape(n_pool, page * N_HEADS, HEAD_DIM),
                                 cache_v.reshape(n_pool, page * N_HEADS, HEAD_DIM), page_table)
    o_sb = jnp.concatenate([o_sb_p, o_sb_s.astype(BF16), jnp.zeros((tp - n_tok, width), BF16)], axis=0)

    z_blk = conv_w // width
    og_p, ssm_p = gated_deltanet(
        rest, 0, rest, z_blk, ba, jnp.zeros((bp, CONV_TAPS - 1, conv_w), F32),
        jnp.zeros((bp, N_HEADS, HEAD_DIM, HEAD_DIM), F32), w_conv[0], a_log[0], dt_bias[0], g_onorm[0],
        batch=bp, seq=sp, chunk=GDN_CHUNK, valid_rows=GDN_CHUNK)
    pad_rows = GDN_DECODE_ROWS
    spread = lambda t: jnp.zeros((bs, pad_rows, t.shape[1]), F32).at[:, 0].set(t).reshape(bs * pad_rows, t.shape[1])
    rest_s = rest[n_prompt:n_tok]
    conv_in_s = rest_s[:, :conv_w]
    og_s, ssm_s = gated_deltanet(
        spread(conv_in_s), 0, spread(rest_s[:, conv_w:conv_w + width]), 0, spread(ba[n_prompt:n_tok]),
        state_conv[0].astype(F32), state_ssm[0].astype(F32), w_conv[0], a_log[0], dt_bias[0], g_onorm[0],
        batch=bs, seq=pad_rows, chunk=pad_rows, valid_rows=1)
    o_g = jnp.concatenate([og_p, og_s.reshape(bs, pad_rows, width)[:, 0], jnp.zeros((tp - n_tok, width), BF16)], axis=0)
    tail = CONV_TAPS - 1
    conv_prompt = jnp.stack([rest[(b + 1) * sp - tail:(b + 1) * sp, :conv_w] for b in range(bp)])[None]
    conv_sample = jnp.concatenate([state_conv[0][:, 1:].astype(F32), conv_in_s[:, None]], axis=1)[None]

    gate_blk = (conv_w + width) // MM_COL_TILE
    merged = gated_merge(o_sb, o_g, rest, gate_blk, gate_blk + d // MM_COL_TILE,
                         w_o_sb[0].astype(BF16), w_o_gdn[0].astype(BF16))
    h1, u2, top_i, top_w = outproj_router(x_p, x_s, merged, w_out[0].astype(BF16), g_ffn[0], w_router[0], b_router[0])

    te, tr, src_tok, dest = routing_tables(top_i, n_tok, n_experts, MOE_TILE_ROWS)
    y_rows = routed_experts(u2, te, tr, src_tok, w_up[0], b_up[0], w_down[0], b_down[0])

    dest_tok = jnp.zeros((tp, TOP_K), I32).at[:n_tok].set(dest.reshape(n_tok, TOP_K))
    dest_tiles = dest_tok.reshape(tp // ROW_TILE, ROW_TILE, TOP_K).transpose(0, 2, 1).reshape(tp // ROW_TILE, 1, TOP_K * ROW_TILE)
    p_all = jnp.concatenate([p_prompt[0].reshape(n_prompt, -1), p_sample[0].reshape(bs, -1),
                             jnp.zeros((tp - n_tok, p_prompt.shape[-1]), p_prompt.dtype)], axis=0)
    y_p, y_s = combine_ple_final(y_rows, dest_tiles, h1, top_w, p_all, g_ple[0], w_ple_gate[0].astype(BF16),
                                 w_ple_proj[0].astype(BF16), g_final, n_prompt)

    y_prompt = y_p.reshape(bp, sp, d)
    y_sample = y_s[:bs].reshape(bs, 1, d)
    return (y_prompt, y_sample, k_prompt, v_prompt, k_sample, v_sample,
            conv_prompt, ssm_p[None], conv_sample, ssm_s[None])
```

```python
import functools

import jax
import jax.numpy as jnp
from jax import lax
from jax.experimental import pallas as pl
from jax.experimental.pallas import tpu as pltpu

F32 = jnp.float32
BF16 = jnp.bfloat16
I32 = jnp.int32

EPS = 1e-6
LANES = 128
HEAD_DIM = 128
N_HEADS = 8
CONV_TAPS = 4
GDN_CHUNK = 64
GDN_DECODE_ROWS = 16
TOP_K = 4
SWIGLU_ALPHA = 1.702
SWIGLU_LIMIT = 7.0
VMEM_LIMIT = 56 * 1024 * 1024

ROW_TILE = 256
MM_ROW_TILE = 512
MM_COL_TILE = 1024
ATT_TILE = 512
ATT_HEAD_GROUP = 4
DECODE_PAGE_GROUP = 16
MOE_TILE_ROWS = 1280
MOE_ROW_STEPS = (512, 1024, 1280)
MOE_FF_TILE = 256
DMA_UNROLL = 8


def _cparams(sem):
    return pltpu.CompilerParams(dimension_semantics=sem, vmem_limit_bytes=VMEM_LIMIT)


def _sigmoid(x):
    return 1.0 / (1.0 + jnp.exp(-x))


def _softplus_neg_abs(x):
    return jnp.log(1.0 + jnp.exp(-jnp.abs(x)))


def _split2(x):
    hi = x.astype(BF16)
    lo = (x - hi.astype(F32)).astype(BF16)
    return hi, lo


def _split3(x):
    hi = x.astype(BF16)
    r = x - hi.astype(F32)
    mid = r.astype(BF16)
    lo = (r - mid.astype(F32)).astype(BF16)
    return hi, mid, lo


def _dot(a, b):
    return jnp.dot(a, b, preferred_element_type=F32)


def _dot_nt(a, b):
    return lax.dot_general(a, b, (((1,), (1,)), ((), ())), preferred_element_type=F32)


def _dot_tn(a, b):
    return lax.dot_general(a, b, (((0,), (0,)), ((), ())), preferred_element_type=F32)


def _dot_wide(a, b):
    ah, al = _split2(a)
    bh, bl = _split2(b)
    return _dot(ah, bh) + (_dot(ah, bl) + _dot(al, bh))


def _dot_exact_lhs(a_bf16, x):
    hi, mid, lo = _split3(x)
    return _dot(a_bf16, hi) + (_dot(a_bf16, mid) + _dot(a_bf16, lo))


def _rms(x, g):
    return x * lax.rsqrt(jnp.mean(x * x, axis=-1, keepdims=True) + EPS) * g


def _two_source_specs(tm, d, n_a):
    return [pl.BlockSpec((tm, d), lambda i: (jnp.minimum(i, n_a - 1), 0)),
            pl.BlockSpec((tm, d), lambda i: (jnp.maximum(i - n_a, 0), 0))]


def _two_source_tile(xa_ref, xb_ref, n_a):
    return jnp.where(pl.program_id(0) < n_a, xa_ref[...], xb_ref[...])


def _rms_kernel(xa_ref, xb_ref, g_ref, o_ref, *, n_a):
    o_ref[...] = _rms(_two_source_tile(xa_ref, xb_ref, n_a), g_ref[...]).astype(o_ref.dtype)


def rmsnorm_cast(xa, xb, g, tm):
    d = xa.shape[1]
    n_a, n_b = xa.shape[0] // tm, xb.shape[0] // tm
    return pl.pallas_call(
        functools.partial(_rms_kernel, n_a=n_a),
        grid=(n_a + n_b,),
        in_specs=_two_source_specs(tm, d, n_a) + [pl.BlockSpec((1, d), lambda i: (0, 0))],
        out_specs=pl.BlockSpec((tm, d), lambda i: (i, 0)),
        out_shape=jax.ShapeDtypeStruct(((n_a + n_b) * tm, d), BF16),
        compiler_params=_cparams(("arbitrary",)),
        name="rmsnorm_cast",
    )(xa, xb, g.reshape(1, d))


def _mm_kernel(a_ref, w_ref, *o_refs):
    acc = _dot(a_ref[...], w_ref[...])
    for o_ref in o_refs:
        o_ref[...] = acc.astype(o_ref.dtype)


def matmul(a, w, tm, tn, name, out_dtypes=(F32,), row_start=0, rows=None, col_start=0, cols=None):
    k = a.shape[1]
    n = w.shape[1] if cols is None else cols
    rows = a.shape[0] if rows is None else rows
    first = row_start // tm
    first_col = col_start // tn
    outs = pl.pallas_call(
        _mm_kernel,
        grid=(n // tn, rows // tm),
        in_specs=[pl.BlockSpec((tm, k), lambda j, i: (first + i, 0)),
                  pl.BlockSpec((k, tn), lambda j, i: (0, first_col + j))],
        out_specs=[pl.BlockSpec((tm, tn), lambda j, i: (i, j)) for _ in out_dtypes],
        out_shape=[jax.ShapeDtypeStruct((rows, n), dt) for dt in out_dtypes],
        compiler_params=_cparams(("arbitrary", "arbitrary")),
        name=name,
    )(a, w)
    return outs[0] if len(outs) == 1 else outs


def _suffix_matrix():
    j = lax.broadcasted_iota(I32, (LANES, 2 * LANES), 0)
    s = lax.broadcasted_iota(I32, (LANES, 2 * LANES), 1)
    return jnp.where((s >= LANES) | (j > s), 1.0, 0.0).astype(BF16)


def _sb_prompt_kernel(bias_ref, q_ref, k_ref, v_ref, cum_ref, o_ref, *, tile, scale, group):
    hg = pl.program_id(1)
    i = pl.program_id(2)
    heads = range(group)
    col = lambda h: slice(h * HEAD_DIM, (h + 1) * HEAD_DIM)
    q = [q_ref[:, col(h)] for h in heads]
    bias = [bias_ref[hg * group + h] for h in heads]
    cum = cum_ref[...]
    rows = lax.broadcasted_iota(I32, (tile, LANES), 0)
    cols = lax.broadcasted_iota(I32, (tile, LANES), 1)
    nsub = tile // LANES

    def block(j, carry, diagonal):
        run, acc = carry
        k0 = pl.multiple_of(j * tile, tile)
        s = [_dot_nt(q[h], k_ref[pl.ds(k0, tile), col(h)]) * scale + bias[h] for h in heads]
        for sb in reversed(range(nsub)):
            sub = [s[h][:, sb * LANES:(sb + 1) * LANES] for h in heads]
            t = [_softplus_neg_abs(x) for x in sub]
            log_beta = [jnp.minimum(x, 0.0) - tt for x, tt in zip(sub, t)]
            log_keep = [lb - x for lb, x in zip(log_beta, sub)]
            if diagonal:
                vis = (cols + sb * LANES) < rows
                log_keep = [jnp.where(vis, x, 0.0) for x in log_keep]
            c = [_dot(x.astype(BF16), cum) for x in log_keep]
            w = [jnp.exp(log_beta[h] + (c[h][:, :LANES] + run[h])) for h in heads]
            if diagonal:
                w = [jnp.where(vis, x, 0.0) for x in w]
            run = [run[h] + c[h][:, LANES:] for h in heads]
            ks = pl.multiple_of(k0 + sb * LANES, LANES)
            acc = [acc[h] + _dot(w[h].astype(BF16), v_ref[pl.ds(ks, LANES), col(h)]) for h in heads]
        return run, acc

    zero = [jnp.zeros((tile, LANES), F32) for _ in heads]
    carry = block(i, (zero, zero), True)
    _, acc = lax.fori_loop(1, i + 1, lambda jj, c: block(i - jj, c, False), carry)
    for h in heads:
        o_ref[:, col(h)] = acc[h].astype(o_ref.dtype)


def sb_attention_prompt(q, k, v, bias, batch, seq):
    tile = ATT_TILE
    group = ATT_HEAD_GROUP
    nq = seq // tile
    gw = group * HEAD_DIM
    kern = functools.partial(_sb_prompt_kernel, tile=tile, scale=HEAD_DIM ** -0.5, group=group)
    return pl.pallas_call(
        kern,
        grid=(batch, N_HEADS // group, nq),
        in_specs=[
            pl.BlockSpec(memory_space=pltpu.SMEM),
            pl.BlockSpec((tile, gw), lambda b, g, i: (b * nq + i, g)),
            pl.BlockSpec((seq, gw), lambda b, g, i: (b, g)),
            pl.BlockSpec((seq, gw), lambda b, g, i: (b, g)),
            pl.BlockSpec((LANES, 2 * LANES), lambda b, g, i: (0, 0)),
        ],
        out_specs=pl.BlockSpec((tile, gw), lambda b, g, i: (b * nq + i, g)),
        out_shape=jax.ShapeDtypeStruct((batch * seq, N_HEADS * HEAD_DIM), BF16),
        compiler_params=_cparams(("arbitrary", "arbitrary", "arbitrary")),
        name="sb_attention_prompt",
    )(bias, q, k, v, _suffix_matrix())


def _sb_decode_kernel(pt_ref, qt_ref, bias_ref, *refs, scale, n_steps, group, page):
    k_refs, v_refs = refs[:group], refs[group:2 * group]
    cumt_ref, expand_ref, o_ref, z_scr, run_scr, acc_scr = refs[2 * group:]
    jj = pl.program_id(1)

    @pl.when(jj == 0)
    def _():
        run_scr[...] = jnp.zeros_like(run_scr)
        acc_scr[...] = jnp.zeros_like(acc_scr)

    pages = range(group)
    lane = lax.broadcasted_iota(I32, (page, LANES), 1)
    qt = qt_ref[0]
    bias = bias_ref[...]
    for g in pages:
        z_scr[g] = _dot(k_refs[g][0].astype(BF16), qt)
    s = []
    for g in pages:
        acc = jnp.zeros((page, LANES), F32)
        for h in range(N_HEADS):
            acc = jnp.where(lane == h, z_scr[g, pl.ds(h, page, stride=N_HEADS), :], acc)
        s.append(acc * scale + bias)
    t = [_softplus_neg_abs(x) for x in s]
    log_beta = [jnp.minimum(x, 0.0) - tt for x, tt in zip(s, t)]
    log_keep = [lb - x for lb, x in zip(log_beta, s)]
    c = [_dot(cumt_ref[...], x.astype(BF16)) for x in log_keep]
    run = run_scr[...]
    w = []
    for g in pages:
        w.append(jnp.exp(log_beta[g] + (c[g][:page] + run)))
        run = run + c[g][page:]
    run_scr[...] = run
    wide = [_dot(x.astype(BF16), expand_ref[...]) for x in w]
    for h in range(N_HEADS):
        tot = acc_scr[h]
        for g in pages:
            vh = v_refs[g][0, pl.ds(h, page, stride=N_HEADS), :]
            prod = wide[g][:, h * HEAD_DIM:(h + 1) * HEAD_DIM] * vh
            tot = tot + jnp.sum(prod.reshape(page // 8, 8, HEAD_DIM), axis=0)
        acc_scr[h] = tot

    @pl.when(jj == n_steps - 1)
    def _():
        o_ref[0] = jnp.concatenate([jnp.sum(acc_scr[h], axis=0, keepdims=True) for h in range(N_HEADS)], axis=0)


def sb_attention_decode(q, bias, cache_k, cache_v, page_table):
    b, width = q.shape
    n_pages = page_table.shape[1]
    page = cache_k.shape[1] // N_HEADS
    group = DECODE_PAGE_GROUP
    assert page == LANES and n_pages % group == 0 and cache_k.shape[2] == HEAD_DIM
    n_steps = n_pages // group
    qt = jnp.zeros((b, HEAD_DIM, LANES), F32).at[:, :, :N_HEADS].set(
        q.reshape(b, N_HEADS, HEAD_DIM).transpose(0, 2, 1)).astype(BF16)
    bias_row = jnp.zeros((1, LANES), F32).at[0, :N_HEADS].set(bias)
    r = lax.broadcasted_iota(I32, (2 * page, page), 0)
    cidx = lax.broadcasted_iota(I32, (2 * page, page), 1)
    cumt = jnp.where((r >= page) | (cidx > r), 1.0, 0.0).astype(BF16)
    lane = lax.broadcasted_iota(I32, (LANES, width), 0)
    col_head = lax.broadcasted_iota(I32, (LANES, width), 1) // HEAD_DIM
    expand = jnp.where(lane == col_head, 1.0, 0.0).astype(BF16)

    def page_spec(g):
        return pl.BlockSpec((1, page * N_HEADS, HEAD_DIM),
                            lambda i, j, pt: (pt[i, n_pages - 1 - (j * group + g)], 0, 0))

    kern = functools.partial(_sb_decode_kernel, scale=HEAD_DIM ** -0.5, n_steps=n_steps, group=group, page=page)
    out = pl.pallas_call(
        kern,
        grid_spec=pltpu.PrefetchScalarGridSpec(
            num_scalar_prefetch=1,
            grid=(b, n_steps),
            in_specs=[pl.BlockSpec((1, HEAD_DIM, LANES), lambda i, j, pt: (i, 0, 0)),
                      pl.BlockSpec((1, LANES), lambda i, j, pt: (0, 0))]
                     + [page_spec(g) for g in range(group)] * 2
                     + [pl.BlockSpec((2 * page, page), lambda i, j, pt: (0, 0)),
                        pl.BlockSpec((LANES, width), lambda i, j, pt: (0, 0))],
            out_specs=pl.BlockSpec((1, N_HEADS, HEAD_DIM), lambda i, j, pt: (i, 0, 0)),
            scratch_shapes=[pltpu.VMEM((group, page * N_HEADS, LANES), F32), pltpu.VMEM((page, LANES), F32),
                            pltpu.VMEM((N_HEADS, 8, HEAD_DIM), F32)],
        ),
        out_shape=jax.ShapeDtypeStruct((b, N_HEADS, HEAD_DIM), F32),
        compiler_params=_cparams(("arbitrary", "arbitrary")),
        name="sb_attention_decode",
    )(page_table, qt, bias_row, *([cache_k] * group), *([cache_v] * group), cumt, expand)
    return out.reshape(b, width)


def _inv_unit_lower(mats, size):
    rr = lax.broadcasted_iota(I32, mats[0].shape, 0)
    cc = lax.broadcasted_iota(I32, mats[0].shape, 1)
    eye = jnp.where(rr == cc, 1.0, 0.0)
    t = [eye - n for n in mats]
    p = list(mats)
    span = 2
    while span < size:
        p = [_dot_wide(x, x) for x in p]
        t = [x + _dot_wide(x, y) for x, y in zip(t, p)]
        span *= 2
    return t


def _gdn_kernel(xc_ref, z_ref, ba_ref, hist_ref, s0_ref, wconv_ref, arow_ref, dtrow_ref, gon_ref, lmat_ref,
                og_ref, sout_ref, xp_scr, s_scr, *, chunk, n_chunks, valid_rows):
    c = pl.program_id(1)
    qk_w = N_HEADS * HEAD_DIM

    @pl.when(c == 0)
    def _():
        xp_scr[5:8, :] = hist_ref[0]
        s_scr[...] = s0_ref[0]

    x = xc_ref[...]
    xp_scr[8:8 + chunk, :] = x
    wc = wconv_ref[...]
    conv = xp_scr[5:5 + chunk, :] * wc[0:1, :]
    conv = conv + xp_scr[6:6 + chunk, :] * wc[1:2, :]
    conv = conv + xp_scr[7:7 + chunk, :] * wc[2:3, :]
    conv = conv + x * wc[3:4, :]
    xp_scr[5:8, :] = x[chunk - 3:chunk, :]
    conv = conv * _sigmoid(conv)

    ba = ba_ref[...]
    beta_all = _sigmoid(ba)
    sp = ba + dtrow_ref[...]
    g_all = -arow_ref[...] * (jnp.maximum(sp, 0.0) + _softplus_neg_abs(sp))
    if valid_rows < chunk:
        live = lax.broadcasted_iota(I32, ba.shape, 0) < valid_rows
        beta_all = jnp.where(live, beta_all, 0.0)
        g_all = jnp.where(live, g_all, 0.0)
    stacked = _dot_exact_lhs(lmat_ref[...], g_all)
    gc_all = stacked[:chunk]
    gc_t = stacked[:LANES].T
    g_tot = stacked[LANES:]

    ii = lax.broadcasted_iota(I32, (chunk, chunk), 0)
    jj = lax.broadcasted_iota(I32, (chunk, chunk), 1)
    incl = ii >= jj
    strict = ii > jj
    gon = gon_ref[...]

    heads = range(N_HEADS)
    col = lambda h, part: slice(part * qk_w + h * HEAD_DIM, part * qk_w + (h + 1) * HEAD_DIM)
    l2n = lambda x: x * lax.rsqrt(jnp.sum(x * x, axis=-1, keepdims=True) + EPS)
    q = [l2n(conv[:, col(h, 0)]) * (HEAD_DIM ** -0.5) for h in heads]
    k = [l2n(conv[:, col(h, 1)]) for h in heads]
    v = [conv[:, col(h, 2)] for h in heads]
    gate_lane = lambda h: slice(N_HEADS + h, N_HEADS + h + 1)
    bcol = [beta_all[:, h:h + 1] for h in heads]
    gcol = [gc_all[:, gate_lane(h)] for h in heads]
    grow = [gc_t[gate_lane(h), :chunk] for h in heads]
    glast_c = [g_tot[:chunk, gate_lane(h)] for h in heads]
    glast_s = [g_tot[:, gate_lane(h)] for h in heads]
    decay = [jnp.where(incl, jnp.exp(jnp.where(incl, gcol[h] - grow[h], 0.0)), 0.0) for h in heads]
    kb = [k[h] * bcol[h] for h in heads]
    k_b16 = [x.astype(BF16) for x in k]
    n_mat = [jnp.where(strict, _dot_nt(kb[h].astype(BF16), k_b16[h]) * decay[h], 0.0) for h in heads]
    t_inv = _inv_unit_lower(n_mat, chunk)
    rhs = [jnp.concatenate([v[h] * bcol[h], kb[h] * jnp.exp(gcol[h])], axis=1) for h in heads]
    sol = [_dot_wide(t_inv[h], rhs[h]) for h in heads]
    qk = [_dot_nt(q[h].astype(BF16), k_b16[h]) * decay[h] for h in heads]
    q_dec = [(q[h] * jnp.exp(gcol[h])).astype(BF16) for h in heads]
    k_end = [(k[h] * jnp.exp(glast_c[h] - gcol[h])).astype(BF16) for h in heads]
    state = [s_scr[h] for h in heads]
    state_b = [x.astype(BF16) for x in state]
    v_new = [sol[h][:, :HEAD_DIM] - _dot(sol[h][:, HEAD_DIM:].astype(BF16), state_b[h]) for h in heads]
    v_new_b = [x.astype(BF16) for x in v_new]
    o = [_dot(q_dec[h], state_b[h]) + _dot(qk[h].astype(BF16), v_new_b[h]) for h in heads]
    new_state = [state[h] * jnp.exp(glast_s[h]) + _dot_tn(k_end[h], v_new_b[h]) for h in heads]
    for h in heads:
        s_scr[h] = new_state[h]
    for h in heads:
        zh = z_ref[:, col(h, 0)]
        og_ref[:, col(h, 0)] = (_rms(o[h], gon) * (zh * _sigmoid(zh))).astype(og_ref.dtype)

    @pl.when(c == n_chunks - 1)
    def _():
        sout_ref[0] = s_scr[...]


def gated_deltanet(xc, xc_col, zz, z_col, ba, hist, s0, w_conv, a_log, dt_bias, g_onorm, *,
                   batch, seq, chunk, valid_rows):
    n_chunks = seq // chunk
    width = N_HEADS * HEAD_DIM
    a_row = jnp.zeros((1, LANES), F32).at[0, N_HEADS:2 * N_HEADS].set(jnp.exp(a_log.astype(F32)))
    dt_row = jnp.zeros((1, LANES), F32).at[0, N_HEADS:2 * N_HEADS].set(dt_bias.astype(F32))
    r = lax.broadcasted_iota(I32, (2 * LANES, chunk), 0)
    t = lax.broadcasted_iota(I32, (2 * LANES, chunk), 1)
    lmat = jnp.where((r >= LANES) | (t <= r), 1.0, 0.0).astype(BF16)
    kern = functools.partial(_gdn_kernel, chunk=chunk, n_chunks=n_chunks, valid_rows=valid_rows)
    return pl.pallas_call(
        kern,
        grid=(batch, n_chunks),
        in_specs=[
            pl.BlockSpec((chunk, 3 * width), lambda b, c: (b * n_chunks + c, xc_col)),
            pl.BlockSpec((chunk, width), lambda b, c: (b * n_chunks + c, z_col)),
            pl.BlockSpec((chunk, LANES), lambda b, c: (b * n_chunks + c, 0)),
            pl.BlockSpec((1, CONV_TAPS - 1, 3 * width), lambda b, c: (b, 0, 0)),
            pl.BlockSpec((1, N_HEADS, HEAD_DIM, HEAD_DIM), lambda b, c: (b, 0, 0, 0)),
            pl.BlockSpec((CONV_TAPS, 3 * width), lambda b, c: (0, 0)),
            pl.BlockSpec((1, LANES), lambda b, c: (0, 0)),
            pl.BlockSpec((1, LANES), lambda b, c: (0, 0)),
            pl.BlockSpec((1, HEAD_DIM), lambda b, c: (0, 0)),
            pl.BlockSpec((2 * LANES, chunk), lambda b, c: (0, 0)),
        ],
        out_specs=[
            pl.BlockSpec((chunk, width), lambda b, c: (b * n_chunks + c, 0)),
            pl.BlockSpec((1, N_HEADS, HEAD_DIM, HEAD_DIM), lambda b, c: (b, 0, 0, 0)),
        ],
        out_shape=[
            jax.ShapeDtypeStruct((batch * seq, width), BF16),
            jax.ShapeDtypeStruct((batch, N_HEADS, HEAD_DIM, HEAD_DIM), F32),
        ],
        scratch_shapes=[pltpu.VMEM((8 + chunk, 3 * width), F32), pltpu.VMEM((N_HEADS, HEAD_DIM, HEAD_DIM), F32)],
        compiler_params=_cparams(("arbitrary", "arbitrary")),
        name="gated_deltanet",
    )(xc, zz, ba, hist, s0, w_conv, a_row, dt_row, g_onorm.reshape(1, HEAD_DIM), lmat)


def _merge_kernel(a_ref, b_ref, ga_ref, gb_ref, wa_ref, wb_ref, o_ref):
    ya = _dot(a_ref[...], wa_ref[...])
    yb = _dot(b_ref[...], wb_ref[...])
    o_ref[...] = (_sigmoid(ga_ref[...]) * ya + _sigmoid(gb_ref[...]) * yb).astype(o_ref.dtype)


def gated_merge(a, b, proj, ga_col, gb_col, wa, wb):
    m, k = a.shape
    n = wa.shape[1]
    tm, tn = MM_ROW_TILE, MM_COL_TILE
    return pl.pallas_call(
        _merge_kernel,
        grid=(n // tn, m // tm),
        in_specs=[
            pl.BlockSpec((tm, k), lambda j, i: (i, 0)),
            pl.BlockSpec((tm, k), lambda j, i: (i, 0)),
            pl.BlockSpec((tm, tn), lambda j, i: (i, ga_col + j)),
            pl.BlockSpec((tm, tn), lambda j, i: (i, gb_col + j)),
            pl.BlockSpec((k, tn), lambda j, i: (0, j)),
            pl.BlockSpec((k, tn), lambda j, i: (0, j)),
        ],
        out_specs=pl.BlockSpec((tm, tn), lambda j, i: (i, j)),
        out_shape=jax.ShapeDtypeStruct((m, n), BF16),
        compiler_params=_cparams(("arbitrary", "arbitrary")),
        name="gated_merge",
    )(a, b, proj, proj, wa, wb)


def _outproj_router_kernel(xa_ref, xb_ref, mg_ref, wout_ref, g_ref, wrh_ref, wrl_ref, br_ref,
                           h_ref, u_ref, ti_ref, tw_ref, *, n_experts, n_a):
    h1 = _two_source_tile(xa_ref, xb_ref, n_a) + _dot(mg_ref[...], wout_ref[...])
    h_ref[...] = h1
    u = _rms(h1, g_ref[...])
    u_ref[...] = u
    uh, ul = _split2(u)
    logits = _dot(uh, wrh_ref[...]) + (_dot(ul, wrh_ref[...]) + _dot(uh, wrl_ref[...])) + br_ref[...]
    lane = lax.broadcasted_iota(I32, logits.shape, 1).astype(F32)
    vals = jnp.where(lane < n_experts, logits, -jnp.inf)
    top_v, top_i = [], []
    for _ in range(TOP_K):
        m = jnp.max(vals, axis=-1, keepdims=True)
        idx = jnp.min(jnp.where(vals == m, lane, float(LANES)), axis=-1, keepdims=True)
        top_v.append(m)
        top_i.append(idx)
        vals = jnp.where(lane == idx, -jnp.inf, vals)
    ex = [jnp.exp(v - top_v[0]) for v in top_v]
    den = ex[0]
    for e in ex[1:]:
        den = den + e
    tw = jnp.zeros(logits.shape, F32)
    ti = jnp.zeros(logits.shape, F32)
    for k in range(TOP_K):
        tw = jnp.where(lane == k, ex[k] / den, tw)
        ti = jnp.where(lane == k, top_i[k], ti)
    tw_ref[...] = tw
    ti_ref[...] = ti.astype(I32)


def outproj_router(xa, xb, merged, w_out, g_ffn, w_router, b_router):
    m, d = merged.shape
    n_experts = w_router.shape[1]
    tm = ROW_TILE
    n_a = xa.shape[0] // tm
    wr = jnp.zeros((d, LANES), F32).at[:, :n_experts].set(w_router.astype(F32))
    wr_hi = wr.astype(BF16)
    wr_lo = (wr - wr_hi.astype(F32)).astype(BF16)
    br = jnp.zeros((1, LANES), F32).at[0, :n_experts].set(b_router.astype(F32))
    row = lambda i: (i, 0)
    fixed = lambda i: (0, 0)
    kern = functools.partial(_outproj_router_kernel, n_experts=n_experts, n_a=n_a)
    return pl.pallas_call(
        kern,
        grid=(m // tm,),
        in_specs=_two_source_specs(tm, d, n_a) + [
            pl.BlockSpec((tm, d), row), pl.BlockSpec((d, d), fixed),
            pl.BlockSpec((1, d), fixed), pl.BlockSpec((d, LANES), fixed), pl.BlockSpec((d, LANES), fixed),
            pl.BlockSpec((1, LANES), fixed),
        ],
        out_specs=[pl.BlockSpec((tm, d), row), pl.BlockSpec((tm, d), row),
                   pl.BlockSpec((tm, LANES), row), pl.BlockSpec((tm, LANES), row)],
        out_shape=[jax.ShapeDtypeStruct((m, d), F32), jax.ShapeDtypeStruct((m, d), F32),
                   jax.ShapeDtypeStruct((m, LANES), I32), jax.ShapeDtypeStruct((m, LANES), F32)],
        compiler_params=_cparams(("arbitrary",)),
        name="outproj_router",
    )(xa, xb, merged, w_out, g_ffn.reshape(1, d), wr_hi, wr_lo, br)


def _row_copy(src_hbm, dst, src_row, dst_row, sem):
    return pltpu.make_async_copy(src_hbm.at[pl.ds(src_row, 1)], dst.at[pl.ds(dst_row, 1)], sem)


def _moe_kernel(te_ref, tr_ref, idx_ref, u_hbm, wg_ref, wl_ref, bg_ref, bl_ref, wd_ref, bd_ref,
                y_ref, xg_scr, xb_scr, sem, *, row_steps):
    s = pl.program_id(0)
    f = pl.program_id(1)
    n_rows = tr_ref[s]

    @pl.when((s == 0) & (f == 0))
    def _():
        xg_scr[...] = jnp.zeros_like(xg_scr)

    @pl.when(n_rows > 0)
    def _():
        @pl.when(f == 0)
        def _():
            n_groups = (n_rows + DMA_UNROLL - 1) // DMA_UNROLL

            def issue(gi, carry):
                for k in range(DMA_UNROLL):
                    r = gi * DMA_UNROLL + k
                    _row_copy(u_hbm, xg_scr, idx_ref[0, 0, r], r, sem).start()
                return carry

            lax.fori_loop(0, n_groups, issue, 0)

            def drain(gi, carry):
                for k in range(DMA_UNROLL):
                    _row_copy(u_hbm, xg_scr, 0, 0, sem).wait()
                return carry

            lax.fori_loop(0, n_groups, drain, 0)
            xb_scr[...] = xg_scr[...].astype(BF16)
            y_ref[...] = jnp.broadcast_to(bd_ref[0], y_ref.shape)

    def expert_pass(m):
        xb = xb_scr[0:m, :]
        glu = jnp.minimum(_dot(xb, wg_ref[0].astype(BF16)) + bg_ref[0], SWIGLU_LIMIT)
        lin = jnp.clip(_dot(xb, wl_ref[0].astype(BF16)) + bl_ref[0], -SWIGLU_LIMIT, SWIGLU_LIMIT)
        act = glu * _sigmoid(SWIGLU_ALPHA * glu) * (lin + 1.0)
        y_ref[0:m, :] += _dot(act.astype(BF16), wd_ref[0].astype(BF16))

    lower = 0
    for m in row_steps:
        @pl.when((n_rows > lower) & (n_rows <= m))
        def _(m=m):
            expert_pass(m)
        lower = m

    @pl.when((n_rows == 0) & (f == 0))
    def _():
        y_ref[...] = jnp.zeros_like(y_ref)


def routed_experts(u, tile_expert, tile_rows, src_tok, w_up, b_up, w_down, b_down):
    n_tiles, _, rows = src_tok.shape
    n_exp, d, two_ff = w_up.shape
    d_ff = two_ff // 2
    tf = MOE_FF_TILE
    nf = d_ff // tf

    def ff(s, f, tr):
        return jnp.where(tr[s] > 0, f, nf - 1)

    assert MOE_ROW_STEPS[-1] == rows
    kern = functools.partial(_moe_kernel, row_steps=MOE_ROW_STEPS)
    return pl.pallas_call(
        kern,
        grid_spec=pltpu.PrefetchScalarGridSpec(
            num_scalar_prefetch=2,
            grid=(n_tiles, nf),
            in_specs=[
                pl.BlockSpec((1, 1, rows), lambda s, f, te, tr: (s, 0, 0), memory_space=pltpu.SMEM),
                pl.BlockSpec(memory_space=pl.ANY),
                pl.BlockSpec((1, d, tf), lambda s, f, te, tr: (te[s], 0, ff(s, f, tr))),
                pl.BlockSpec((1, d, tf), lambda s, f, te, tr: (te[s], 0, nf + ff(s, f, tr))),
                pl.BlockSpec((1, 1, tf), lambda s, f, te, tr: (te[s], 0, ff(s, f, tr))),
                pl.BlockSpec((1, 1, tf), lambda s, f, te, tr: (te[s], 0, nf + ff(s, f, tr))),
                pl.BlockSpec((1, tf, d), lambda s, f, te, tr: (te[s], ff(s, f, tr), 0)),
                pl.BlockSpec((1, 1, d), lambda s, f, te, tr: (te[s], 0, 0)),
            ],
            out_specs=pl.BlockSpec((rows, d), lambda s, f, te, tr: (s, 0)),
            scratch_shapes=[
                pltpu.VMEM((rows, d), F32), pltpu.VMEM((rows, d), BF16), pltpu.SemaphoreType.DMA(()),
            ],
        ),
        out_shape=jax.ShapeDtypeStruct((n_tiles * rows, d), F32),
        compiler_params=_cparams(("arbitrary", "arbitrary")),
        name="routed_experts",
    )(tile_expert, tile_rows, src_tok, u, w_up, w_up,
      b_up.reshape(n_exp, 1, two_ff), b_up.reshape(n_exp, 1, two_ff), w_down, b_down.reshape(n_exp, 1, d))


def routing_tables(top_idx, n_tokens, n_experts, rows):
    n_assign = n_tokens * TOP_K
    n_tiles = n_assign // rows + n_experts
    e_flat = top_idx[:n_tokens, :TOP_K].reshape(-1)
    onehot = (e_flat[:, None] == jnp.arange(n_experts, dtype=I32)[None, :]).astype(I32)
    csum = jnp.cumsum(onehot, axis=0)
    count = csum[-1]
    pos = jnp.sum(csum * onehot, axis=1) - 1
    tiles_of = (count + rows - 1) // rows
    tile_end = jnp.cumsum(tiles_of)
    tile_start = tile_end - tiles_of
    used = tile_end[-1]
    dest = tile_start[e_flat] * rows + pos
    tile_id = jnp.arange(n_tiles, dtype=I32)
    last = used - 1
    tile_block = jnp.minimum(tile_id, last)
    tile_expert = jnp.sum((tile_end[None, :] <= tile_block[:, None]).astype(I32), axis=1)
    tile_rows = jnp.clip(count[tile_expert] - (tile_id - tile_start[tile_expert]) * rows, 0, rows)
    tile_rows = jnp.where(tile_id < used, tile_rows, 0)
    token_of = jnp.arange(n_assign, dtype=I32) // TOP_K
    src_tok = jnp.zeros((n_tiles * rows,), I32).at[dest].set(token_of).reshape(n_tiles, 1, rows)
    return tile_expert.astype(I32), tile_rows.astype(I32), src_tok, dest.astype(I32)


def _final_kernel(dst_ref, dnext_ref, y_hbm, h_ref, tw_ref, p_ref, gple_ref, wgate_ref, wproj_ref, gfin_ref,
                  oa_ref, ob_ref, buf, sems, *, tm, n_tiles, n_a):
    i = pl.program_id(0)
    slot = lax.rem(i, 2)
    rows_per_trip = DMA_UNROLL // TOP_K

    def fetch(table_ref, to_slot):
        def issue(gi, carry):
            for u in range(rows_per_trip):
                r = gi * rows_per_trip + u
                for k in range(TOP_K):
                    _row_copy(y_hbm, buf.at[to_slot, k], table_ref[0, 0, k * tm + r], r, sems.at[to_slot]).start()
            return carry

        lax.fori_loop(0, tm // rows_per_trip, issue, 0)

    @pl.when(i == 0)
    def _():
        fetch(dst_ref, 0)

    @pl.when(i + 1 < n_tiles)
    def _():
        fetch(dnext_ref, 1 - slot)

    def drain(gi, carry):
        for _ in range(DMA_UNROLL):
            _row_copy(y_hbm, buf.at[slot, 0], 0, 0, sems.at[slot]).wait()
        return carry

    lax.fori_loop(0, TOP_K * tm // DMA_UNROLL, drain, 0)
    tw = tw_ref[...]
    h2 = h_ref[...]
    for k in range(TOP_K):
        h2 = h2 + tw[:, k:k + 1] * buf[slot, k]
    un = _rms(h2, gple_ref[...]).astype(BF16)
    gate = _sigmoid(_dot(un, wgate_ref[...]))
    h3 = h2 + gate * _dot(p_ref[...].astype(BF16), wproj_ref[...])
    y = _rms(h3, gfin_ref[...])

    @pl.when(i < n_a)
    def _():
        oa_ref[...] = y

    @pl.when(i >= n_a)
    def _():
        ob_ref[...] = y


def combine_ple_final(y_rows, dest, h1, top_w, p, g_ple, w_gate, w_proj, g_final, rows_a):
    m, d = h1.shape
    tm = ROW_TILE
    pd = p.shape[1]
    row = lambda i: (i, 0)
    fixed = lambda i: (0, 0)
    n_tiles = m // tm
    n_a = rows_a // tm
    kern = functools.partial(_final_kernel, tm=tm, n_tiles=n_tiles, n_a=n_a)
    return pl.pallas_call(
        kern,
        grid=(n_tiles,),
        in_specs=[
            pl.BlockSpec((1, 1, TOP_K * tm), lambda i: (i, 0, 0), memory_space=pltpu.SMEM),
            pl.BlockSpec((1, 1, TOP_K * tm), lambda i: (jnp.minimum(i + 1, n_tiles - 1), 0, 0),
                         memory_space=pltpu.SMEM),
            pl.BlockSpec(memory_space=pl.ANY),
            pl.BlockSpec((tm, d), row), pl.BlockSpec((tm, LANES), row), pl.BlockSpec((tm, pd), row),
            pl.BlockSpec((1, d), fixed), pl.BlockSpec((d, d), fixed), pl.BlockSpec((pd, d), fixed),
            pl.BlockSpec((1, d), fixed),
        ],
        out_specs=[pl.BlockSpec((tm, d), lambda i: (jnp.minimum(i, n_a - 1), 0)),
                   pl.BlockSpec((tm, d), lambda i: (jnp.maximum(i - n_a, 0), 0))],
        out_shape=[jax.ShapeDtypeStruct((n_a * tm, d), F32), jax.ShapeDtypeStruct(((n_tiles - n_a) * tm, d), F32)],
        scratch_shapes=[pltpu.VMEM((2, TOP_K, tm, d), F32), pltpu.SemaphoreType.DMA((2,))],
        compiler_params=_cparams(("arbitrary",)),
        name="combine_ple_final",
    )(dest, dest, y_rows, h1, top_w, p, g_ple.reshape(1, d), w_gate, w_proj, g_final.reshape(1, d))


def kernel(x_prompt, x_sample, cache_k, cache_v, page_table, state_conv, state_ssm, p_prompt, p_sample, g_mix, w_in, b_sb, w_conv, a_log, dt_bias, g_onorm, w_o_sb, w_o_gdn, w_out, g_ffn, w_router, b_router, w_up, b_up, w_down, b_down, g_ple, w_ple_gate, w_ple_proj, g_final):
    bp, sp, d = x_prompt.shape
    bs = x_sample.shape[0]
    assert x_sample.shape[1] == 1 and g_mix.shape[0] == 1
    n_prompt = bp * sp
    n_tok = n_prompt + bs
    tp = -(-n_tok // MM_ROW_TILE) * MM_ROW_TILE
    width = N_HEADS * HEAD_DIM
    conv_w = 3 * width
    n_experts = w_router.shape[2]

    rest_lo = 3 * width
    ba_lo = rest_lo + conv_w + width
    ba_hi = ba_lo + 2 * N_HEADS
    w_cols = lambda lo, hi: w_in[0, :, lo:hi].astype(BF16)
    w_rest = jnp.concatenate([w_cols(rest_lo, ba_lo), w_cols(ba_hi, w_in.shape[2])], axis=1)
    w_ba = jnp.zeros((d, LANES), BF16).at[:, :2 * N_HEADS].set(w_cols(ba_lo, ba_hi))

    x_p = x_prompt.reshape(n_prompt, d)
    x_s = jnp.zeros((tp - n_prompt, d), x_sample.dtype).at[:bs].set(x_sample.reshape(bs, d))
    xn = rmsnorm_cast(x_p, x_s, g_mix[0], ROW_TILE)
    tm = MM_ROW_TILE
    w_qkv = w_cols(0, rest_lo)
    q_b = matmul(xn, w_qkv, tm, MM_COL_TILE, "in_proj_q", (BF16,), 0, n_prompt, 0, width)
    k_f, k_b = matmul(xn, w_qkv, tm, MM_COL_TILE, "in_proj_k", (F32, BF16), 0, n_prompt, width, width)
    v_f, v_b = matmul(xn, w_qkv, tm, MM_COL_TILE, "in_proj_v", (F32, BF16), 0, n_prompt, 2 * width, width)
    qkv_s = matmul(xn, w_qkv, tm, MM_COL_TILE, "in_proj_qkv_sample", (F32,), n_prompt, tp - n_prompt)
    rest = matmul(xn, w_rest, tm, 2 * MM_COL_TILE, "in_proj_rest")
    ba = matmul(xn, w_ba, tm, LANES, "in_proj_gates")

    heads = lambda t, b, s: t.reshape(1, b, s, N_HEADS, HEAD_DIM)
    k_prompt = heads(k_f, bp, sp)
    v_prompt = heads(v_f, bp, sp)
    k_sample = heads(qkv_s[:bs, width:2 * width], bs, 1)
    v_sample = heads(qkv_s[:bs, 2 * width:rest_lo], bs, 1)

    o_sb_p = sb_attention_prompt(q_b, k_b, v_b, b_sb[0].astype(F32), bp, sp)
    n_pool, page = cache_k.shape[1], cache_k.shape[2]
    o_sb_s = sb_attention_decode(qkv_s[:bs, :width], b_sb[0].astype(F32),
                                 cache_k.reshape(n_pool, page * N_HEADS, HEAD_DIM),
                                 cache_v.reshape(n_pool, page * N_HEADS, HEAD_DIM), page_table)
    o_sb = jnp.concatenate([o_sb_p, o_sb_s.astype(BF16), jnp.zeros((tp - n_tok, width), BF16)], axis=0)

    z_blk = conv_w // width
    og_p, ssm_p = gated_deltanet(
        rest, 0, rest, z_blk, ba, jnp.zeros((bp, CONV_TAPS - 1, conv_w), F32),
        jnp.zeros((bp, N_HEADS, HEAD_DIM, HEAD_DIM), F32), w_conv[0], a_log[0], dt_bias[0], g_onorm[0],
        batch=bp, seq=sp, chunk=GDN_CHUNK, valid_rows=GDN_CHUNK)
    pad_rows = GDN_DECODE_ROWS
    spread = lambda t: jnp.zeros((bs, pad_rows, t.shape[1]), F32).at[:, 0].set(t).reshape(bs * pad_rows, t.shape[1])
    rest_s = rest[n_prompt:n_tok]
    conv_in_s = rest_s[:, :conv_w]
    og_s, ssm_s = gated_deltanet(
        spread(conv_in_s), 0, spread(rest_s[:, conv_w:conv_w + width]), 0, spread(ba[n_prompt:n_tok]),
        state_conv[0].astype(F32), state_ssm[0].astype(F32), w_conv[0], a_log[0], dt_bias[0], g_onorm[0],
        batch=bs, seq=pad_rows, chunk=pad_rows, valid_rows=1)
    o_g = jnp.concatenate([og_p, og_s.reshape(bs, pad_rows, width)[:, 0], jnp.zeros((tp - n_tok, width), BF16)], axis=0)
    tail = CONV_TAPS - 1
    conv_prompt = jnp.stack([rest[(b + 1) * sp - tail:(b + 1) * sp, :conv_w] for b in range(bp)])[None]
    conv_sample = jnp.concatenate([state_conv[0][:, 1:].astype(F32), conv_in_s[:, None]], axis=1)[None]

    gate_blk = (conv_w + width) // MM_COL_TILE
    merged = gated_merge(o_sb, o_g, rest, gate_blk, gate_blk + d // MM_COL_TILE,
                         w_o_sb[0].astype(BF16), w_o_gdn[0].astype(BF16))
    h1, u2, top_i, top_w = outproj_router(x_p, x_s, merged, w_out[0].astype(BF16), g_ffn[0], w_router[0], b_router[0])

    te, tr, src_tok, dest = routing_tables(top_i, n_tok, n_experts, MOE_TILE_ROWS)
    y_rows = routed_experts(u2, te, tr, src_tok, w_up[0], b_up[0], w_down[0], b_down[0])

    dest_tok = jnp.zeros((tp, TOP_K), I32).at[:n_tok].set(dest.reshape(n_tok, TOP_K))
    dest_tiles = dest_tok.reshape(tp // ROW_TILE, ROW_TILE, TOP_K).transpose(0, 2, 1).reshape(tp // ROW_TILE, 1, TOP_K * ROW_TILE)
    p_all = jnp.concatenate([p_prompt[0].reshape(n_prompt, -1), p_sample[0].reshape(bs, -1),
                             jnp.zeros((tp - n_tok, p_prompt.shape[-1]), p_prompt.dtype)], axis=0)
    y_p, y_s = combine_ple_final(y_rows, dest_tiles, h1, top_w, p_all, g_ple[0], w_ple_gate[0].astype(BF16),
                                 w_ple_proj[0].astype(BF16), g_final, n_prompt)

    y_prompt = y_p.reshape(bp, sp, d)
    y_sample = y_s[:bs].reshape(bs, 1, d)
    return (y_prompt, y_sample, k_prompt, v_prompt, k_sample, v_sample,
            conv_prompt, ssm_p[None], conv_sample, ssm_s[None])
```
